```python
import math
import jax, jax.numpy as jnp
from jax import lax
import numpy as np

D_MODEL = 1024
BATCH = 4
SEQ = 4096
DEPTH = 2

GRID_W = 64
CTX_LEN = 256
N_HEADS = 8
HEAD_DIM = 64
V_DIM = 2 * HEAD_DIM
QK_W = N_HEADS * 2 * HEAD_DIM
ATTN_W = N_HEADS * V_DIM
CONV_W = 1024
CONV_K = 31
ROT_AXIS = HEAD_DIM // 2
ROPE_THETA = 10000.0
Q_BLOCK = 128
D_FF = 2816
N_EXPERTS = 8
TOP_K = 2
D_FF_EXPERT = 3584
N_DENSE = (DEPTH + 1) // 2
N_MOE = DEPTH // 2
EPS = 1e-6
Q_OFF = 0
K_OFF = Q_OFF + QK_W
V_OFF = K_OFF + QK_W
GLU_OFF = V_OFF + ATTN_W
GATE_OFF = GLU_OFF + 2 * CONV_W
IN_W = GATE_OFF + 2 * D_MODEL

kernel_name = "hybrid_diffattn_conformer_moe_dit"


def _rms(x, g):
    xf = x.astype(jnp.float32)
    y = xf * lax.rsqrt(jnp.mean(xf * xf, axis=-1, keepdims=True) + EPS)
    return (y * g.astype(jnp.float32)).astype(x.dtype)


def _layer_norm(x, g, b):
    xf = x.astype(jnp.float32)
    mu = jnp.mean(xf, axis=-1, keepdims=True)
    xc = xf - mu
    var = jnp.mean(xc * xc, axis=-1, keepdims=True)
    y = xc * lax.rsqrt(var + EPS) * g.astype(jnp.float32) + b.astype(jnp.float32)
    return y.astype(x.dtype)


def _modulate(h, shift, scale):
    return h * (1 + scale) + shift


def _axial_rope(rows):
    row = jnp.broadcast_to(jnp.arange(rows, dtype=jnp.float32)[:, None], (rows, GRID_W)).reshape(-1)
    col = jnp.broadcast_to(jnp.arange(GRID_W, dtype=jnp.float32)[None, :], (rows, GRID_W)).reshape(-1)
    inv = ROPE_THETA ** (-jnp.arange(0, ROT_AXIS, 2, dtype=jnp.float32) / ROT_AXIS)
    ar = row[:, None] * inv
    ac = col[:, None] * inv
    ang = jnp.concatenate([ar, ar, ac, ac], axis=-1)
    return jnp.cos(ang), jnp.sin(ang)


def _apply_rope(x, cos, sin):
    xs = x.reshape(x.shape[:-1] + (2, 2, ROT_AXIS // 2))
    rot = jnp.concatenate([-xs[..., 1:2, :], xs[..., 0:1, :]], axis=-2).reshape(x.shape)
    c = cos[None, :, None, None, :]
    s = sin[None, :, None, None, :]
    return (x * c + rot * s).astype(x.dtype)


def _qk_heads(raw, g):
    b, n = raw.shape[:2]
    return _rms(raw.reshape(b, n, N_HEADS, 2, HEAD_DIM), g)


def _v_heads(raw):
    b, n = raw.shape[:2]
    return raw.reshape(b, n, N_HEADS, V_DIM)


def _diff_attend(q, k, v, lam):
    s = jnp.einsum('bqhcd,bkhcd->bhcqk', q, k, preferred_element_type=jnp.float32) * (HEAD_DIM ** -0.5)
    p = jax.nn.softmax(s, axis=-1)
    w = p[:, :, 0] - lam * p[:, :, 1]
    return jnp.einsum('bhqk,bkhd->bqhd', w.astype(v.dtype), v)


def _latent_diff_attention(q, k_all, v_all, lam):
    b, s = q.shape[:2]
    nb = s // Q_BLOCK
    qb = jnp.moveaxis(q.reshape((b, nb, Q_BLOCK) + q.shape[2:]), 1, 0)
    ob = lax.map(lambda qq: _diff_attend(qq, k_all, v_all, lam), qb)
    return jnp.moveaxis(ob, 0, 1).reshape(b, s, N_HEADS, V_DIM)


def _attn_out(o, g, lam_init, w_o):
    b, n = o.shape[:2]
    o = _rms(o, g) * (1 - lam_init)
    return o.reshape(b, n, ATTN_W) @ w_o


def _conformer_conv(u_glu, dw_w, dw_b, ln_g, ln_b, w_o):
    a, gt = u_glu[..., :CONV_W], u_glu[..., CONV_W:]
    u = a * jax.nn.sigmoid(gt)
    u = lax.conv_general_dilated(u, dw_w[:, None, :].astype(u.dtype), window_strides=(1,),
                                 padding=[(CONV_K // 2, CONV_K // 2)],
                                 dimension_numbers=('NWC', 'WIO', 'NWC'),
                                 feature_group_count=CONV_W) + dw_b
    u = jax.nn.silu(_layer_norm(u, ln_g, ln_b))
    return u @ w_o


def _merge(p, y_attn, dw_w, dw_b, ln_g, ln_b, w_conv_o, w_out):
    y_conv = _conformer_conv(p[..., GLU_OFF:GATE_OFF], dw_w, dw_b, ln_g, ln_b, w_conv_o)
    gates = jax.nn.sigmoid(p[..., GATE_OFF:])
    return (gates[..., :D_MODEL] * y_attn + gates[..., D_MODEL:] * y_conv) @ w_out


def _swiglu(h, wg, wu, wd):
    return (jax.nn.silu(h @ wg) * (h @ wu)) @ wd


def _moe_swiglu(h, w_r, wg, wu, wd):
    lead = h.shape[:-1]
    t = h.reshape(-1, D_MODEL)
    logits = (t @ w_r).astype(jnp.float32)
    top_v, top_i = lax.top_k(logits, TOP_K)
    top_w = jax.nn.softmax(top_v, axis=-1)
    gates = jnp.sum(jax.nn.one_hot(top_i, N_EXPERTS, dtype=jnp.float32) * top_w[..., None], axis=1)
    out = jnp.zeros_like(t)
    for e in range(N_EXPERTS):
        out = out + gates[:, e:e + 1].astype(t.dtype) * _swiglu(t, wg[e], wu[e], wd[e])
    return out.reshape(lead + (D_MODEL,))


def _ffn(h, l, w_ff_gate, w_ff_up, w_ff_down, w_router, w_exp_gate, w_exp_up, w_exp_down):
    i = l // 2
    if l % 2 == 0:
        return _swiglu(h, w_ff_gate[i], w_ff_up[i], w_ff_down[i])
    return _moe_swiglu(h, w_router[i], w_exp_gate[i], w_exp_up[i], w_exp_down[i])


def setup_inputs(seed: int = 0) -> dict:
    key = jax.random.key(seed)
    ks = jax.random.split(key, 32)
    f = jnp.float32
    nrm = lambda k, shp, s: jax.random.normal(k, shp, f) * s
    gain = lambda k, shp: 1.0 + 0.02 * jax.random.normal(k, shp, f)
    d = D_MODEL
    return {
        "x": nrm(ks[0], (BATCH, SEQ, d), 1.0),
        "c": nrm(ks[1], (BATCH, d), 1.0),
        "ctx": nrm(ks[2], (BATCH, CTX_LEN, d), 1.0),
        "c_ctx": nrm(ks[3], (d,), 1.0),
        "w_mod": nrm(ks[4], (DEPTH, d, 6 * d), 0.5 * d ** -0.5),
        "b_mod": nrm(ks[5], (DEPTH, 6 * d), 0.01),
        "g_mix": gain(ks[6], (DEPTH, d)),
        "w_in": nrm(ks[7], (DEPTH, d, IN_W), d ** -0.5),
        "q_norm_g": gain(ks[8], (DEPTH, HEAD_DIM)),
        "k_norm_g": gain(ks[9], (DEPTH, HEAD_DIM)),
        "lambda_q1": nrm(ks[10], (DEPTH, HEAD_DIM), 0.1),
        "lambda_k1": nrm(ks[11], (DEPTH, HEAD_DIM), 0.1),
        "lambda_q2": nrm(ks[12], (DEPTH, HEAD_DIM), 0.1),
        "lambda_k2": nrm(ks[13], (DEPTH, HEAD_DIM), 0.1),
        "subln_g": gain(ks[14], (DEPTH, V_DIM)),
        "w_attn_o": nrm(ks[15], (DEPTH, ATTN_W, d), ATTN_W ** -0.5),
        "dw_weight": nrm(ks[16], (DEPTH, CONV_K, CONV_W), CONV_K ** -0.5),
        "dw_bias": nrm(ks[17], (DEPTH, CONV_W), 0.01),
        "conv_ln_g": gain(ks[18], (DEPTH, CONV_W)),
        "conv_ln_b": nrm(ks[19], (DEPTH, CONV_W), 0.01),
        "w_conv_o": nrm(ks[20], (DEPTH, CONV_W, d), CONV_W ** -0.5),
        "w_out": nrm(ks[21], (DEPTH, d, d), d ** -0.5),
        "g_ffn": gain(ks[22], (DEPTH, d)),
        "w_ff_gate": nrm(ks[23], (N_DENSE, d, D_FF), d ** -0.5),
        "w_ff_up": nrm(ks[24], (N_DENSE, d, D_FF), d ** -0.5),
        "w_ff_down": nrm(ks[25], (N_DENSE, D_FF, d), D_FF ** -0.5),
        "w_router": nrm(ks[26], (N_MOE, d, N_EXPERTS), d ** -0.5),
        "w_exp_gate": nrm(ks[27], (N_MOE, N_EXPERTS, d, D_FF_EXPERT), d ** -0.5),
        "w_exp_up": nrm(ks[28], (N_MOE, N_EXPERTS, d, D_FF_EXPERT), d ** -0.5),
        "w_exp_down": nrm(ks[29], (N_MOE, N_EXPERTS, D_FF_EXPERT, d), D_FF_EXPERT ** -0.5),
    }


def reference(x, c, ctx, c_ctx, w_mod, b_mod, g_mix, w_in, q_norm_g, k_norm_g,
              lambda_q1, lambda_k1, lambda_q2, lambda_k2, subln_g, w_attn_o,
              dw_weight, dw_bias, conv_ln_g, conv_ln_b, w_conv_o, w_out, g_ffn,
              w_ff_gate, w_ff_up, w_ff_down, w_router, w_exp_gate, w_exp_up, w_exp_down):
    s = x.shape[1]
    rows = s // GRID_W
    cos, sin = _axial_rope(rows)
    xc = ctx
    f32 = jnp.float32
    for l in range(DEPTH):
        last = l == DEPTH - 1
        ml = jnp.split((jax.nn.silu(c) @ w_mod[l] + b_mod[l])[:, None, :], 6, axis=-1)
        mc = jnp.split(jax.nn.silu(c_ctx) @ w_mod[l] + b_mod[l], 6, axis=-1)
        lam_init = 0.8 - 0.6 * math.exp(-0.3 * l)
        lam = (jnp.exp(jnp.sum(lambda_q1[l].astype(f32) * lambda_k1[l].astype(f32)))
               - jnp.exp(jnp.sum(lambda_q2[l].astype(f32) * lambda_k2[l].astype(f32))) + lam_init)

        h_lat = _modulate(_rms(x, g_mix[l]), ml[0], ml[1])
        h_ctx = _modulate(_rms(xc, g_mix[l]), mc[0], mc[1])
        p_lat = h_lat @ w_in[l]
        if last:
            p_kv = h_ctx @ w_in[l][:, K_OFF:GLU_OFF]
            k_ctx_raw, v_ctx_raw = p_kv[..., :QK_W], p_kv[..., QK_W:]
        else:
            p_ctx = h_ctx @ w_in[l]
            k_ctx_raw, v_ctx_raw = p_ctx[..., K_OFF:V_OFF], p_ctx[..., V_OFF:GLU_OFF]
        k_ctx = _qk_heads(k_ctx_raw, k_norm_g[l])
        v_ctx = _v_heads(v_ctx_raw)
        q_lat = _apply_rope(_qk_heads(p_lat[..., Q_OFF:K_OFF], q_norm_g[l]), cos, sin)
        k_lat = _apply_rope(_qk_heads(p_lat[..., K_OFF:V_OFF], k_norm_g[l]), cos, sin)
        v_lat = _v_heads(p_lat[..., V_OFF:GLU_OFF])
        k_all = jnp.concatenate([k_ctx, k_lat], axis=1)
        v_all = jnp.concatenate([v_ctx, v_lat], axis=1)
        o_lat = _latent_diff_attention(q_lat, k_all, v_all, lam)
        y_attn_lat = _attn_out(o_lat, subln_g[l], lam_init, w_attn_o[l])
        mix_lat = _merge(p_lat, y_attn_lat, dw_weight[l], dw_bias[l], conv_ln_g[l], conv_ln_b[l], w_conv_o[l], w_out[l])
        x = x + ml[2] * mix_lat
        if not last:
            q_ctx = _qk_heads(p_ctx[..., Q_OFF:K_OFF], q_norm_g[l])
            o_ctx = _diff_attend(q_ctx, k_ctx, v_ctx, lam)
            y_attn_ctx = _attn_out(o_ctx, subln_g[l], lam_init, w_attn_o[l])
            mix_ctx = _merge(p_ctx, y_attn_ctx, dw_weight[l], dw_bias[l], conv_ln_g[l], conv_ln_b[l], w_conv_o[l], w_out[l])
            xc = xc + mc[2] * mix_ctx

        h2 = _modulate(_rms(x, g_ffn[l]), ml[3], ml[4])
        x = x + ml[5] * _ffn(h2, l, w_ff_gate, w_ff_up, w_ff_down, w_router, w_exp_gate, w_exp_up, w_exp_down)
        if not last:
            h2c = _modulate(_rms(xc, g_ffn[l]), mc[3], mc[4])
            xc = xc + mc[5] * _ffn(h2c, l, w_ff_gate, w_ff_up, w_ff_down, w_router, w_exp_gate, w_exp_up, w_exp_down)
    return x
```

```python
import functools
import math

import jax
import jax.numpy as jnp
from jax import lax
from jax.experimental import pallas as pl
from jax.experimental.pallas import tpu as pltpu

F32 = jnp.float32
BF16 = jnp.bfloat16

LANES = 128
D_MODEL = 1024
N_HEADS = 8
HEAD_DIM = 64
HEAD_W = 2 * HEAD_DIM
GRID_W = 64
ROT_AXIS = HEAD_DIM // 2
ROT_HALF = ROT_AXIS // 2
ROPE_THETA = 10000.0
CONV_K = 31
CONV_HALO = 16
N_EXPERTS = 8
EPS = 1e-6
N_GROUPS = 7
VMEM_LIMIT = 52 * 1024 * 1024
Q_SCALE = (HEAD_DIM ** -0.5) * math.log2(math.e)
NEG_BIG = -1e30


def _cparams(sem):
    return pltpu.CompilerParams(dimension_semantics=sem, vmem_limit_bytes=VMEM_LIMIT)


def _silu(v):
    return v * jax.nn.sigmoid(v)


def _norm_modulate(xf, g, shift, scale):
    ms = jnp.mean(xf * xf, axis=-1, keepdims=True)
    return (xf * lax.rsqrt(ms + EPS) * g) * (1.0 + scale) + shift


def _mod_kernel(c_ref, w_ref, b_ref, o_ref):
    a = _silu(c_ref[...])
    o_ref[...] = jnp.dot(a, w_ref[...], preferred_element_type=F32,
                         precision=lax.Precision.HIGHEST) + b_ref[...]


def _modulation(cvec, w, b):
    rows, d = cvec.shape
    n = w.shape[1]
    tn = n // 4
    return pl.pallas_call(
        _mod_kernel,
        grid=(n // tn,),
        in_specs=[pl.BlockSpec((rows, d), lambda j: (0, 0)),
                  pl.BlockSpec((d, tn), lambda j: (0, j)),
                  pl.BlockSpec((1, tn), lambda j: (0, j))],
        out_specs=pl.BlockSpec((rows, tn), lambda j: (0, j)),
        out_shape=jax.ShapeDtypeStruct((rows, n), F32),
        compiler_params=_cparams(("arbitrary",)),
        name="modulation",
    )(cvec, w, b.reshape(1, n))


def _in_kernel(*refs, g0, ng, rope):
    x_ref, mod_ref, gmix_ref, w_ref, qg_ref, kg_ref, gsum_ref = refs[:7]
    pos = 7
    if rope:
        tq_refs = refs[pos:pos + 3]
        tk_refs = refs[pos + 3:pos + 6]
        pos += 6
    full = ng == N_GROUPS
    if full:
        q_ref, k_ref, v_ref, u_ref, gates_ref = refs[pos:pos + 5]
        pos += 5
    else:
        k_ref, v_ref = refs[pos:pos + 2]
        pos += 2
    h_ref = refs[pos]
    j = pl.program_id(1) + g0

    @pl.when(pl.program_id(1) == 0)
    def _():
        h = _norm_modulate(x_ref[...], gmix_ref[...], mod_ref[:, 0:D_MODEL],
                           mod_ref[:, D_MODEL:2 * D_MODEL])
        h_ref[...] = h.astype(BF16)

    acc = jnp.dot(h_ref[...], w_ref[...], preferred_element_type=F32)

    def qk_norm(out_ref, g_row_ref, tables, const_scale):
        gsum = gsum_ref[...]
        for hb in range(N_HEADS):
            sl = slice(hb * HEAD_W, (hb + 1) * HEAD_W)
            blk = acc[:, sl]
            sq = blk * blk
            hi = sq.astype(BF16)
            lo = (sq - hi.astype(F32)).astype(BF16)
            ss = (jnp.dot(hi, gsum, preferred_element_type=F32)
                  + jnp.dot(lo, gsum, preferred_element_type=F32))
            y = blk * lax.rsqrt(ss * (1.0 / HEAD_DIM) + EPS) * g_row_ref[...]
            if tables is not None:
                c_ref, sa_ref, sb_ref = tables
                y = (y * c_ref[...] + pltpu.roll(y, HEAD_W - ROT_HALF, 1) * sa_ref[...]
                     + pltpu.roll(y, ROT_HALF, 1) * sb_ref[...])
            elif const_scale != 1.0:
                y = y * const_scale
            out_ref[:, sl] = y.astype(BF16)

    if full:
        @pl.when(j == 0)
        def _():
            qk_norm(q_ref, qg_ref, tq_refs if rope else None, Q_SCALE)

    @pl.when(j == 1)
    def _():
        qk_norm(k_ref, kg_ref, tk_refs if rope else None, 1.0)

    @pl.when(j == 2)
    def _():
        v_ref[...] = acc.astype(BF16)

    if full:
        @pl.when(j == 3)
        def _():
            u_ref[...] = acc

        @pl.when(j == 4)
        def _():
            u_ref[...] = u_ref[...] * jax.nn.sigmoid(acc)

        @pl.when(j == 5)
        def _():
            gates_ref[:, 0:D_MODEL] = jax.nn.sigmoid(acc).astype(BF16)

        @pl.when(j == 6)
        def _():
            gates_ref[:, D_MODEL:2 * D_MODEL] = jax.nn.sigmoid(acc).astype(BF16)


def _in_proj(x, mod3, row_of, gmix, w_bf, qg, kg, gsum, rope_tabs, seq, g0, ng, tm):
    m, d = x.shape
    tps = seq // tm
    rope = rope_tabs is not None
    full = ng == N_GROUPS
    in_specs = [
        pl.BlockSpec((tm, d), lambda i, j: (i, 0)),
        pl.BlockSpec((None, 1, mod3.shape[2]), lambda i, j: (row_of(i), 0, 0)),
        pl.BlockSpec((1, d), lambda i, j: (0, 0)),
        pl.BlockSpec((d, D_MODEL), lambda i, j: (0, j + g0)),
        pl.BlockSpec((1, HEAD_W), lambda i, j: (0, 0)),
        pl.BlockSpec((1, HEAD_W), lambda i, j: (0, 0)),
        pl.BlockSpec((HEAD_W, HEAD_W), lambda i, j: (0, 0)),
    ]
    args = [x, mod3, gmix, w_bf, qg, kg, gsum]
    if rope:
        for t in rope_tabs:
            in_specs.append(pl.BlockSpec((tm, HEAD_W), lambda i, j: (i % tps, 0)))
            args.append(t)
    row_spec = pl.BlockSpec((tm, D_MODEL), lambda i, j: (i, 0))
    if full:
        out_specs = [row_spec, row_spec, row_spec, row_spec,
                     pl.BlockSpec((tm, 2 * D_MODEL), lambda i, j: (i, 0))]
        out_shape = [jax.ShapeDtypeStruct((m, D_MODEL), BF16)] * 3 + [
            jax.ShapeDtypeStruct((m, D_MODEL), F32),
            jax.ShapeDtypeStruct((m, 2 * D_MODEL), BF16)]
    else:
        out_specs = [row_spec, row_spec]
        out_shape = [jax.ShapeDtypeStruct((m, D_MODEL), BF16)] * 2
    return pl.pallas_call(
        functools.partial(_in_kernel, g0=g0, ng=ng, rope=rope),
        grid=(m // tm, ng),
        in_specs=in_specs,
        out_specs=out_specs,
        out_shape=out_shape,
        scratch_shapes=[pltpu.VMEM((tm, d), BF16)],
        compiler_params=_cparams(("parallel", "arbitrary")),
        name="in_proj",
    )(*args)


def _attn_kernel(*refs, src_lens, tq, tk, lam_init):
    n_src = len(src_lens)
    q_ref = refs[0]
    kv_refs = refs[1:1 + 2 * n_src]
    lamv_ref, sg_ref, o_ref, m_ref, l_ref, acc_ref = refs[1 + 2 * n_src:]

    q = q_ref[...]
    lane = lax.broadcasted_iota(jnp.int32, q.shape, 1)
    zero = jnp.zeros_like(q)
    q2 = jnp.concatenate([jnp.where(lane < HEAD_DIM, q, zero),
                          jnp.where(lane >= HEAD_DIM, q, zero)], axis=0)
    m_ref[...] = jnp.full(m_ref.shape, NEG_BIG, F32)
    l_ref[...] = jnp.zeros(l_ref.shape, F32)
    acc_ref[...] = jnp.zeros(acc_ref.shape, F32)

    for si in range(n_src):
        k_ref, v_ref = kv_refs[2 * si], kv_refs[2 * si + 1]
        tkk = min(tk, src_lens[si])

        def body(c, carry, k_ref=k_ref, v_ref=v_ref, tkk=tkk):
            off = pl.multiple_of(c * tkk, tkk)
            kc = k_ref[pl.ds(off, tkk), :]
            vc = v_ref[pl.ds(off, tkk), :]
            s = lax.dot_general(q2, kc, (((1,), (1,)), ((), ())), preferred_element_type=F32)
            m_prev = m_ref[...]
            m_new = jnp.maximum(m_prev, jnp.max(s, axis=1, keepdims=True))
            alpha = jnp.exp2(m_prev - m_new)
            p = jnp.exp2(s - jnp.tile(m_new, (1, tkk // LANES)))
            l_ref[...] = alpha * l_ref[...] + jnp.sum(p, axis=1, keepdims=True)
            acc_ref[...] = alpha * acc_ref[...] + jnp.dot(p.astype(BF16), vc,
                                                         preferred_element_type=F32)
            m_ref[...] = m_new
            return carry

        lax.fori_loop(0, src_lens[si] // tkk, body, 0)

    lv = lamv_ref[...]
    lam = (jnp.exp(jnp.sum(lv[0:1] * lv[1:2], axis=1, keepdims=True))
           - jnp.exp(jnp.sum(lv[2:3] * lv[3:4], axis=1, keepdims=True)) + lam_init)
    o = acc_ref[0:tq, :] / l_ref[0:tq, :] - lam * (acc_ref[tq:2 * tq, :] / l_ref[tq:2 * tq, :])
    ms = jnp.mean(o * o, axis=1, keepdims=True)
    y = o * lax.rsqrt(ms + EPS) * sg_ref[...] * (1.0 - lam_init)
    o_ref[...] = y.astype(BF16)


def _diff_attention(q, kvs, lamv, sg, n_batch, sq, tq, tk, lam_init):
    m = q.shape[0]
    nq = sq // tq
    in_specs = [pl.BlockSpec((tq, HEAD_W), lambda b, h, i: (b * nq + i, h))]
    args = [q]
    src_lens = []
    for (k, v) in kvs:
        sl = k.shape[0] // n_batch
        src_lens.append(sl)
        for a in (k, v):
            in_specs.append(pl.BlockSpec((sl, HEAD_W), lambda b, h, i: (b, h)))
            args.append(a)
    in_specs += [pl.BlockSpec(lamv.shape, lambda b, h, i: (0, 0)),
                 pl.BlockSpec((1, HEAD_W), lambda b, h, i: (0, 0))]
    args += [lamv, sg]
    return pl.pallas_call(
        functools.partial(_attn_kernel, src_lens=tuple(src_lens), tq=tq, tk=tk, lam_init=lam_init),
        grid=(n_batch, N_HEADS, nq),
        in_specs=in_specs,
        out_specs=pl.BlockSpec((tq, HEAD_W), lambda b, h, i: (b * nq + i, h)),
        out_shape=jax.ShapeDtypeStruct((m, N_HEADS * HEAD_W), BF16),
        scratch_shapes=[pltpu.VMEM((2 * tq, LANES), F32), pltpu.VMEM((2 * tq, LANES), F32),
                        pltpu.VMEM((2 * tq, HEAD_W), F32)],
        compiler_params=_cparams(("parallel", "parallel", "arbitrary")),
        name="diff_attention",
    )(*args)


def _merge_kernel(x_ref, ao_ref, u_ref, up_ref, un_ref, gates_ref, mod_ref, dw_ref, dwb_ref,
                  lng_ref, lnb_ref, wco_ref, wao_ref, wout_ref, o_ref, ubuf, cbuf, *, tm, tps, rc):
    i = pl.program_id(0)
    first = (i % tps) == 0
    last = (i % tps) == tps - 1
    ubuf[0:CONV_HALO, :] = jnp.where(first, 0.0, up_ref[...])
    ubuf[CONV_HALO:CONV_HALO + tm, :] = u_ref[...]
    ubuf[CONV_HALO + tm:2 * CONV_HALO + tm, :] = jnp.where(last, 0.0, un_ref[...])
    shift = CONV_HALO - CONV_K // 2
    win = rc + 2 * CONV_HALO

    def chunk(r, carry):
        r0 = pl.multiple_of(r * rc, rc)
        for cb in range(D_MODEL // LANES):
            sl = slice(cb * LANES, (cb + 1) * LANES)
            w = ubuf[pl.ds(r0, win), sl]
            acc = jnp.zeros((rc, LANES), F32)
            for rot in range(8):
                wr = w if rot == 0 else pltpu.roll(w, win - rot, 0)
                for k in range(CONV_K):
                    if (k + shift) % 8 == rot:
                        a0 = (k + shift) - rot
                        acc = acc + wr[a0:a0 + rc, :] * dw_ref[k:k + 1, sl]
            cbuf[pl.ds(r0, rc), sl] = acc + dwb_ref[:, sl]
        return carry

    lax.fori_loop(0, tm // rc, chunk, 0)

    cv = cbuf[...]
    mu = jnp.mean(cv, axis=-1, keepdims=True)
    xc = cv - mu
    var = jnp.mean(xc * xc, axis=-1, keepdims=True)
    y = xc * lax.rsqrt(var + EPS) * lng_ref[...] + lnb_ref[...]
    y_conv = jnp.dot(_silu(y).astype(BF16), wco_ref[...], preferred_element_type=F32)
    y_attn = jnp.dot(ao_ref[...], wao_ref[...], preferred_element_type=F32)
    mix = (gates_ref[:, 0:D_MODEL].astype(F32) * y_attn
           + gates_ref[:, D_MODEL:2 * D_MODEL].astype(F32) * y_conv)
    out = jnp.dot(mix.astype(BF16), wout_ref[...], preferred_element_type=F32)
    o_ref[...] = x_ref[...] + mod_ref[:, 2 * D_MODEL:3 * D_MODEL] * out


def _merge(x, ao, u, gates, mod3, row_of, dw, dwb, lng, lnb, wco, wao, wout, seq, tm):
    m, d = x.shape
    tps = seq // tm
    hb = tm // CONV_HALO
    n_halo = m // CONV_HALO
    row = lambda i: (i, 0)
    const = lambda i: (0, 0)
    in_specs = [
        pl.BlockSpec((tm, d), row),
        pl.BlockSpec((tm, d), row),
        pl.BlockSpec((tm, d), row),
        pl.BlockSpec((CONV_HALO, d), lambda i: (jnp.maximum(i * hb - 1, 0), 0)),
        pl.BlockSpec((CONV_HALO, d), lambda i: (jnp.minimum((i + 1) * hb, n_halo - 1), 0)),
        pl.BlockSpec((tm, 2 * d), row),
        pl.BlockSpec((None, 1, mod3.shape[2]), lambda i: (row_of(i), 0, 0)),
        pl.BlockSpec(dw.shape, const),
        pl.BlockSpec((1, d), const), pl.BlockSpec((1, d), const), pl.BlockSpec((1, d), const),
        pl.BlockSpec((d, d), const), pl.BlockSpec((d, d), const), pl.BlockSpec((d, d), const),
    ]
    return pl.pallas_call(
        functools.partial(_merge_kernel, tm=tm, tps=tps, rc=32),
        grid=(m // tm,),
        in_specs=in_specs,
        out_specs=pl.BlockSpec((tm, d), row),
        out_shape=jax.ShapeDtypeStruct((m, d), F32),
        scratch_shapes=[pltpu.VMEM((tm + 2 * CONV_HALO, d), F32), pltpu.VMEM((tm, d), F32)],
        compiler_params=_cparams(("parallel",)),
        name="conv_merge",
    )(x, ao, u, u, u, gates, mod3, dw, dwb, lng, lnb, wco, wao, wout)


def _ffn_kernel(x_ref, mod_ref, g_ref, wg_ref, wu_ref, wd_ref, o_ref, h_ref, acc_ref, *, nf):
    f = pl.program_id(1)

    @pl.when(f == 0)
    def _():
        h = _norm_modulate(x_ref[...], g_ref[...], mod_ref[:, 3 * D_MODEL:4 * D_MODEL],
                           mod_ref[:, 4 * D_MODEL:5 * D_MODEL])
        h_ref[...] = h.astype(BF16)
        acc_ref[...] = jnp.zeros(acc_ref.shape, F32)

    hb = h_ref[...]
    a = jnp.dot(hb, wg_ref[...], preferred_element_type=F32)
    b = jnp.dot(hb, wu_ref[...], preferred_element_type=F32)
    acc_ref[...] += jnp.dot((_silu(a) * b).astype(BF16), wd_ref[...], preferred_element_type=F32)

    @pl.when(f == nf - 1)
    def _():
        o_ref[...] = x_ref[...] + mod_ref[:, 5 * D_MODEL:6 * D_MODEL] * acc_ref[...]


def _dense_ffn(x, mod3, row_of, g, wg, wu, wd, tm, tf):
    m, d = x.shape
    dff = wg.shape[1]
    nf = dff // tf
    return pl.pallas_call(
        functools.partial(_ffn_kernel, nf=nf),
        grid=(m // tm, nf),
        in_specs=[pl.BlockSpec((tm, d), lambda i, f: (i, 0)),
                  pl.BlockSpec((None, 1, mod3.shape[2]), lambda i, f: (row_of(i), 0, 0)),
                  pl.BlockSpec((1, d), lambda i, f: (0, 0)),
                  pl.BlockSpec((d, tf), lambda i, f: (0, f)),
                  pl.BlockSpec((d, tf), lambda i, f: (0, f)),
                  pl.BlockSpec((tf, d), lambda i, f: (f, 0))],
        out_specs=pl.BlockSpec((tm, d), lambda i, f: (i, 0)),
        out_shape=jax.ShapeDtypeStruct((m, d), F32),
        scratch_shapes=[pltpu.VMEM((tm, d), BF16), pltpu.VMEM((tm, d), F32)],
        compiler_params=_cparams(("parallel", "arbitrary")),
        name="dense_ffn",
    )(x, mod3, g, wg, wu, wd)


def _top2_gates(logits):
    lane = lax.broadcasted_iota(jnp.int32, logits.shape, 1)
    lg = jnp.where(lane < N_EXPERTS, logits, NEG_BIG)
    m1 = jnp.max(lg, axis=1, keepdims=True)
    i1 = jnp.min(jnp.where(lg == m1, lane, LANES), axis=1, keepdims=True)
    lg2 = jnp.where(lane == i1, NEG_BIG, lg)
    m2 = jnp.max(lg2, axis=1, keepdims=True)
    i2 = jnp.min(jnp.where(lg2 == m2, lane, LANES), axis=1, keepdims=True)
    e2 = jnp.exp(m2 - m1)
    w1 = 1.0 / (1.0 + e2)
    w2 = e2 / (1.0 + e2)
    return jnp.where(lane == i1, w1, 0.0) + jnp.where(lane == i2, w2, 0.0)


def _moe_kernel(x_ref, mod_ref, g_ref, wr_ref, wg_ref, wu_ref, wd_ref, o_ref,
                h_ref, acc_ref, gate_ref, *, nf):
    e = pl.program_id(1)
    f = pl.program_id(2)

    @pl.when((e == 0) & (f == 0))
    def _():
        h = _norm_modulate(x_ref[...], g_ref[...], mod_ref[:, 3 * D_MODEL:4 * D_MODEL],
                           mod_ref[:, 4 * D_MODEL:5 * D_MODEL])
        h_ref[...] = h.astype(BF16)
        acc_ref[...] = jnp.zeros(acc_ref.shape, F32)
        logits = jnp.dot(h, wr_ref[...], preferred_element_type=F32,
                         precision=lax.Precision.HIGHEST)
        gate_ref[...] = _top2_gates(logits)

    hb = h_ref[...]
    a = jnp.dot(hb, wg_ref[...], preferred_element_type=F32)
    b = jnp.dot(hb, wu_ref[...], preferred_element_type=F32)
    y = jnp.dot((_silu(a) * b).astype(BF16), wd_ref[...], preferred_element_type=F32)
    lane = lax.broadcasted_iota(jnp.int32, gate_ref.shape, 1)
    ge = jnp.sum(jnp.where(lane == e, gate_ref[...], 0.0), axis=1, keepdims=True)
    acc_ref[...] += ge * y

    @pl.when((e == N_EXPERTS - 1) & (f == nf - 1))
    def _():
        o_ref[...] = x_ref[...] + mod_ref[:, 5 * D_MODEL:6 * D_MODEL] * acc_ref[...]


def _moe_ffn(x, mod3, row_of, g, wr_pad, wg, wu, wd, tm, tf):
    m, d = x.shape
    dff = wg.shape[2]
    nf = dff // tf
    return pl.pallas_call(
        functools.partial(_moe_kernel, nf=nf),
        grid=(m // tm, N_EXPERTS, nf),
        in_specs=[pl.BlockSpec((tm, d), lambda i, e, f: (i, 0)),
                  pl.BlockSpec((None, 1, mod3.shape[2]), lambda i, e, f: (row_of(i), 0, 0)),
                  pl.BlockSpec((1, d), lambda i, e, f: (0, 0)),
                  pl.BlockSpec((d, LANES), lambda i, e, f: (0, 0)),
                  pl.BlockSpec((None, d, tf), lambda i, e, f: (e, 0, f)),
                  pl.BlockSpec((None, d, tf), lambda i, e, f: (e, 0, f)),
                  pl.BlockSpec((None, tf, d), lambda i, e, f: (e, f, 0))],
        out_specs=pl.BlockSpec((tm, d), lambda i, e, f: (i, 0)),
        out_shape=jax.ShapeDtypeStruct((m, d), F32),
        scratch_shapes=[pltpu.VMEM((tm, d), BF16), pltpu.VMEM((tm, d), F32),
                        pltpu.VMEM((tm, LANES), F32)],
        compiler_params=_cparams(("parallel", "arbitrary", "arbitrary")),
        name="moe_ffn",
    )(x, mod3, g, wr_pad, wg, wu, wd)


def _rope_tables(seq):
    t = jnp.arange(seq, dtype=jnp.int32)
    row = (t // GRID_W).astype(F32)
    col = (t % GRID_W).astype(F32)
    inv = ROPE_THETA ** (-jnp.arange(0, ROT_AXIS, 2, dtype=F32) / ROT_AXIS)
    ar = row[:, None] * inv
    ac = col[:, None] * inv
    ang = jnp.concatenate([ar, ar, ac, ac], axis=-1)
    ang = jnp.concatenate([ang, ang], axis=-1)
    cos, sin = jnp.cos(ang), jnp.sin(ang)
    upper = (jnp.arange(HEAD_W) & ROT_HALF) != 0
    sa = jnp.where(upper, 0.0, -sin)
    sb = jnp.where(upper, sin, 0.0)
    return cos, sa, sb


def _pick_tile(n, pref):
    t = min(n, pref)
    while n % t:
        t //= 2
    return t


def kernel(x, c, ctx, c_ctx, w_mod, b_mod, g_mix, w_in, q_norm_g, k_norm_g, lambda_q1, lambda_k1, lambda_q2, lambda_k2, subln_g, w_attn_o, dw_weight, dw_bias, conv_ln_g, conv_ln_b, w_conv_o, w_out, g_ffn, w_ff_gate, w_ff_up, w_ff_down, w_router, w_exp_gate, w_exp_up, w_exp_down):
    n_batch, seq, d = x.shape
    ctx_len = ctx.shape[1]
    depth = w_mod.shape[0]
    assert d == D_MODEL and seq % GRID_W == 0

    tm_l = _pick_tile(seq, 512)
    tm_c = _pick_tile(ctx_len, 512)
    ctx_row = n_batch
    mod_rows = -(-(n_batch + 1) // 8) * 8
    cvec = jnp.zeros((mod_rows, d), F32).at[:n_batch].set(c).at[ctx_row].set(c_ctx)

    cos, sa, sb = _rope_tables(seq)
    tabs = (cos * Q_SCALE, sa * Q_SCALE, sb * Q_SCALE, cos, sa, sb)
    gid = jnp.arange(HEAD_W) // HEAD_DIM
    gsum = (gid[:, None] == gid[None, :]).astype(BF16)

    xl = x.reshape(n_batch * seq, d)
    xc = ctx.reshape(n_batch * ctx_len, d)
    lat_row = lambda i: (i * tm_l) // seq
    ctx_row_of = lambda i: ctx_row

    for l in range(depth):
        last = l == depth - 1
        lam_init = 0.8 - 0.6 * math.exp(-0.3 * l)
        mod3 = _modulation(cvec, w_mod[l], b_mod[l]).reshape(mod_rows, 1, 6 * d)
        w_bf = w_in[l].astype(BF16)
        gmix = g_mix[l].reshape(1, d)
        qg = jnp.tile(q_norm_g[l], 2).reshape(1, HEAD_W)
        kg = jnp.tile(k_norm_g[l], 2).reshape(1, HEAD_W)
        lamv = jnp.stack([lambda_q1[l], lambda_k1[l], lambda_q2[l], lambda_k2[l]])
        sg = subln_g[l].reshape(1, HEAD_W)
        dw = jnp.zeros((32, d), F32).at[:CONV_K].set(dw_weight[l])
        merge_w = (dw, dw_bias[l].reshape(1, d), conv_ln_g[l].reshape(1, d),
                   conv_ln_b[l].reshape(1, d), w_conv_o[l].astype(BF16),
                   w_attn_o[l].astype(BF16), w_out[l].astype(BF16))

        q, k, v, u, gates = _in_proj(xl, mod3, lat_row, gmix, w_bf, qg, kg, gsum, tabs,
                                     seq, 0, N_GROUPS, tm_l)
        if last:
            kc, vc = _in_proj(xc, mod3, ctx_row_of, gmix, w_bf, qg, kg, gsum, None,
                              ctx_len, 1, 2, tm_c)
        else:
            qc, kc, vc, uc, gatesc = _in_proj(xc, mod3, ctx_row_of, gmix, w_bf, qg, kg, gsum,
                                              None, ctx_len, 0, N_GROUPS, tm_c)
        ao = _diff_attention(q, [(k, v), (kc, vc)], lamv, sg, n_batch, seq,
                             _pick_tile(seq, 256), 512, lam_init)
        xl = _merge(xl, ao, u, gates, mod3, lat_row, *merge_w, seq, tm_l)
        if not last:
            aoc = _diff_attention(qc, [(kc, vc)], lamv, sg, n_batch, ctx_len,
                                  _pick_tile(ctx_len, 256), 512, lam_init)
            xc = _merge(xc, aoc, uc, gatesc, mod3, ctx_row_of, *merge_w, ctx_len, tm_c)

        gf = g_ffn[l].reshape(1, d)
        i = l // 2
        if l % 2 == 0:
            wts = (w_ff_gate[i].astype(BF16), w_ff_up[i].astype(BF16), w_ff_down[i].astype(BF16))
            tf = _pick_ff_tile(wts[0].shape[1])
            xl = _dense_ffn(xl, mod3, lat_row, gf, *wts, tm_l, tf)
            if not last:
                xc = _dense_ffn(xc, mod3, ctx_row_of, gf, *wts, tm_c, tf)
        else:
            wr_pad = jnp.zeros((d, LANES), F32).at[:, :N_EXPERTS].set(w_router[i])
            wts = (w_exp_gate[i].astype(BF16), w_exp_up[i].astype(BF16),
                   w_exp_down[i].astype(BF16))
            tf = _pick_ff_tile(wts[0].shape[2])
            xl = _moe_ffn(xl, mod3, lat_row, gf, wr_pad, *wts, tm_l, tf)
            if not last:
                xc = _moe_ffn(xc, mod3, ctx_row_of, gf, wr_pad, *wts, tm_c, tf)
    return xl.reshape(n_batch, seq, d)


def _pick_ff_tile(dff):
    best = LANES
    for t in range(LANES, 1792 + 1, LANES):
        if dff % t == 0:
            best = t
    return best
```

```python
import functools
import math

import jax
import jax.numpy as jnp
from jax import lax
from jax.experimental import pallas as pl
from jax.experimental.pallas import tpu as pltpu

F32 = jnp.float32
BF16 = jnp.bfloat16

LANES = 128
D_MODEL = 1024
N_HEADS = 8
HEAD_DIM = 64
HEAD_W = 2 * HEAD_DIM
GRID_W = 64
ROT_AXIS = HEAD_DIM // 2
ROT_HALF = ROT_AXIS // 2
ROPE_THETA = 10000.0
CONV_K = 31
CONV_HALO = 16
N_EXPERTS = 8
EPS = 1e-6
N_GROUPS = 7
ATTN_TQ = 256
ATTN_TK = 256
SUM_ROWS = 16
VMEM_LIMIT = 52 * 1024 * 1024
Q_SCALE = (HEAD_DIM ** -0.5) * math.log2(math.e)
NEG_BIG = -1e30


def _cparams(sem):
    return pltpu.CompilerParams(dimension_semantics=sem, vmem_limit_bytes=VMEM_LIMIT)


def _silu(v):
    return v * jax.nn.sigmoid(v)


def _norm_modulate(xf, g, shift, scale):
    ms = jnp.mean(xf * xf, axis=-1, keepdims=True)
    return (xf * lax.rsqrt(ms + EPS) * g) * (1.0 + scale) + shift


def _mod_kernel(c_ref, w_ref, b_ref, o_ref):
    a = _silu(c_ref[...])
    o_ref[...] = jnp.dot(a, w_ref[...], preferred_element_type=F32,
                         precision=lax.Precision.HIGHEST) + b_ref[...]


def _modulation(cvec, w, b):
    rows, d = cvec.shape
    n = w.shape[1]
    tn = n // 4
    return pl.pallas_call(
        _mod_kernel,
        grid=(n // tn,),
        in_specs=[pl.BlockSpec((rows, d), lambda j: (0, 0)),
                  pl.BlockSpec((d, tn), lambda j: (0, j)),
                  pl.BlockSpec((1, tn), lambda j: (0, j))],
        out_specs=pl.BlockSpec((rows, tn), lambda j: (0, j)),
        out_shape=jax.ShapeDtypeStruct((rows, n), F32),
        compiler_params=_cparams(("arbitrary",)),
        name="modulation",
    )(cvec, w, b.reshape(1, n))


def _in_kernel(*refs, g0, ng, rope):
    x_ref, mod_ref, gmix_ref, w_ref, qg_ref, kg_ref, gsum_ref = refs[:7]
    pos = 7
    if rope:
        tq_refs = refs[pos:pos + 3]
        tk_refs = refs[pos + 3:pos + 6]
        pos += 6
    full = ng == N_GROUPS
    if full:
        q_ref, k_ref, v_ref, u_ref, gates_ref = refs[pos:pos + 5]
        pos += 5
    else:
        k_ref, v_ref = refs[pos:pos + 2]
        pos += 2
    h_ref = refs[pos]
    j = pl.program_id(1) + g0

    @pl.when(pl.program_id(1) == 0)
    def _():
        h = _norm_modulate(x_ref[...], gmix_ref[...], mod_ref[:, 0:D_MODEL],
                           mod_ref[:, D_MODEL:2 * D_MODEL])
        h_ref[...] = h.astype(BF16)

    acc = jnp.dot(h_ref[...], w_ref[...], preferred_element_type=F32)

    def qk_norm(out_ref, g_row_ref, tables, const_scale):
        gsum = gsum_ref[...]
        for hb in range(N_HEADS):
            sl = slice(hb * HEAD_W, (hb + 1) * HEAD_W)
            blk = acc[:, sl]
            sq = blk * blk
            hi = sq.astype(BF16)
            lo = (sq - hi.astype(F32)).astype(BF16)
            ss = (jnp.dot(hi, gsum, preferred_element_type=F32)
                  + jnp.dot(lo, gsum, preferred_element_type=F32))
            y = blk * lax.rsqrt(ss * (1.0 / HEAD_DIM) + EPS) * g_row_ref[...]
            if tables is not None:
                c_ref, sa_ref, sb_ref = tables
                y = (y * c_ref[...] + pltpu.roll(y, HEAD_W - ROT_HALF, 1) * sa_ref[...]
                     + pltpu.roll(y, ROT_HALF, 1) * sb_ref[...])
            elif const_scale != 1.0:
                y = y * const_scale
            out_ref[:, sl] = y.astype(BF16)

    if full:
        @pl.when(j == 0)
        def _():
            qk_norm(q_ref, qg_ref, tq_refs if rope else None, Q_SCALE)

    @pl.when(j == 1)
    def _():
        qk_norm(k_ref, kg_ref, tk_refs if rope else None, 1.0)

    @pl.when(j == 2)
    def _():
        v_ref[...] = acc.T.astype(BF16)

    if full:
        @pl.when(j == 3)
        def _():
            u_ref[...] = acc

        @pl.when(j == 4)
        def _():
            u_ref[...] = u_ref[...] * jax.nn.sigmoid(acc)

        @pl.when(j == 5)
        def _():
            gates_ref[:, 0:D_MODEL] = jax.nn.sigmoid(acc).astype(BF16)

        @pl.when(j == 6)
        def _():
            gates_ref[:, D_MODEL:2 * D_MODEL] = jax.nn.sigmoid(acc).astype(BF16)


def _in_proj(x, mod3, row_of, gmix, w_bf, qg, kg, gsum, rope_tabs, seq, g0, ng, tm):
    m, d = x.shape
    tps = seq // tm
    rope = rope_tabs is not None
    full = ng == N_GROUPS
    in_specs = [
        pl.BlockSpec((tm, d), lambda i, j: (i, 0)),
        pl.BlockSpec((None, 1, mod3.shape[2]), lambda i, j: (row_of(i), 0, 0)),
        pl.BlockSpec((1, d), lambda i, j: (0, 0)),
        pl.BlockSpec((d, D_MODEL), lambda i, j: (0, j + g0)),
        pl.BlockSpec((1, HEAD_W), lambda i, j: (0, 0)),
        pl.BlockSpec((1, HEAD_W), lambda i, j: (0, 0)),
        pl.BlockSpec((HEAD_W, HEAD_W), lambda i, j: (0, 0)),
    ]
    args = [x, mod3, gmix, w_bf, qg, kg, gsum]
    if rope:
        for t in rope_tabs:
            in_specs.append(pl.BlockSpec((tm, HEAD_W), lambda i, j: (i % tps, 0)))
            args.append(t)
    row_spec = pl.BlockSpec((tm, D_MODEL), lambda i, j: (i, 0))
    rows_bf = jax.ShapeDtypeStruct((m, D_MODEL), BF16)
    vt_spec = pl.BlockSpec((D_MODEL, tm), lambda i, j: (i // tps, i % tps))
    vt_shape = jax.ShapeDtypeStruct(((m // seq) * D_MODEL, seq), BF16)
    if full:
        out_specs = [row_spec, row_spec, vt_spec, row_spec,
                     pl.BlockSpec((tm, 2 * D_MODEL), lambda i, j: (i, 0))]
        out_shape = [rows_bf, rows_bf, vt_shape, jax.ShapeDtypeStruct((m, D_MODEL), F32),
                     jax.ShapeDtypeStruct((m, 2 * D_MODEL), BF16)]
    else:
        out_specs = [row_spec, vt_spec]
        out_shape = [rows_bf, vt_shape]
    return pl.pallas_call(
        functools.partial(_in_kernel, g0=g0, ng=ng, rope=rope),
        grid=(m // tm, ng),
        in_specs=in_specs,
        out_specs=out_specs,
        out_shape=out_shape,
        scratch_shapes=[pltpu.VMEM((tm, d), BF16)],
        compiler_params=_cparams(("parallel", "arbitrary")),
        name="in_proj",
    )(*args)


def _attn_kernel(*refs, src_lens, tq, tk, lam_init):
    n_src = len(src_lens)
    q_ref = refs[0]
    kv_refs = refs[1:1 + 2 * n_src]
    lamv_ref, sg_ref, o_ref, acc_ref = refs[1 + 2 * n_src:5 + 2 * n_src]
    s_refs = refs[5 + 2 * n_src:7 + 2 * n_src]
    n_cb = 2 * tq // LANES

    q = q_ref[...]
    lane = lax.broadcasted_iota(jnp.int32, q.shape, 1)
    zero = jnp.zeros_like(q)
    q2 = jnp.concatenate([jnp.where(lane < HEAD_DIM, q, zero),
                          jnp.where(lane >= HEAD_DIM, q, zero)], axis=0)
    acc_ref[...] = jnp.zeros(acc_ref.shape, F32)
    ones = jnp.ones((SUM_ROWS, tk), BF16)

    def scores(si, off):
        kc = kv_refs[2 * si][pl.ds(off, tk), :]
        return lax.dot_general(kc, q2, (((1,), (1,)), ((), ())), preferred_element_type=F32)

    def consume(s_ref, si, off, m):
        alphas, ps, m_out = [], [], []
        for cb in range(n_cb):
            s = s_ref[:, cb * LANES:(cb + 1) * LANES]
            m_new = jnp.maximum(m[cb], jnp.max(s, axis=0, keepdims=True))
            alphas.append(jnp.exp2(m[cb] - m_new))
            ps.append(jnp.exp2(s - m_new).astype(BF16))
            m_out.append(m_new)
        vt = jnp.concatenate([kv_refs[2 * si + 1][:, pl.ds(off, tk)], ones], axis=0)
        acc_ref[...] = (jnp.concatenate(alphas, axis=1) * acc_ref[...]
                        + jnp.dot(vt, jnp.concatenate(ps, axis=1), preferred_element_type=F32))
        return tuple(m_out)

    m = tuple(jnp.full((1, LANES), NEG_BIG, F32) for _ in range(n_cb))
    chunks = [(si, j * tk) for si in range(n_src) for j in range(src_lens[si] // tk)]
    s_refs[0][...] = scores(*chunks[0])
    for ci in range(len(chunks)):
        if ci + 1 < len(chunks):
            s_refs[(ci + 1) % 2][...] = scores(*chunks[ci + 1])
        m = consume(s_refs[ci % 2], *chunks[ci], m)

    lv = lamv_ref[...]
    lam = (jnp.exp(jnp.sum(lv[0:1] * lv[1:2], axis=1, keepdims=True))
           - jnp.exp(jnp.sum(lv[2:3] * lv[3:4], axis=1, keepdims=True)) + lam_init)
    inv = 1.0 / acc_ref[HEAD_W:HEAD_W + 1, :]
    acc = acc_ref[0:HEAD_W, :]
    o_t = acc[:, 0:tq] * inv[:, 0:tq] - lam * (acc[:, tq:2 * tq] * inv[:, tq:2 * tq])
    ms = jnp.mean(o_t * o_t, axis=0, keepdims=True)
    y_t = o_t * lax.rsqrt(ms + EPS) * (sg_ref[...] * (1.0 - lam_init))
    o_ref[...] = y_t.T.astype(BF16)


def _diff_attention(q, kvs, lamv, sg_col, n_batch, sq, tq, tk, lam_init):
    m = q.shape[0]
    nq = sq // tq
    in_specs = [pl.BlockSpec((tq, HEAD_W), lambda b, h, i: (b * nq + i, h))]
    args = [q]
    src_lens = []
    for (k, vt) in kvs:
        sl = k.shape[0] // n_batch
        src_lens.append(sl)
        in_specs.append(pl.BlockSpec((sl, HEAD_W), lambda b, h, i: (b, h)))
        in_specs.append(pl.BlockSpec((HEAD_W, sl), lambda b, h, i: (b * N_HEADS + h, 0)))
        args += [k, vt]
    in_specs += [pl.BlockSpec(lamv.shape, lambda b, h, i: (0, 0)),
                 pl.BlockSpec((HEAD_W, 1), lambda b, h, i: (0, 0))]
    args += [lamv, sg_col]
    return pl.pallas_call(
        functools.partial(_attn_kernel, src_lens=tuple(src_lens), tq=tq, tk=tk,
                          lam_init=lam_init),
        grid=(n_batch, N_HEADS, nq),
        in_specs=in_specs,
        out_specs=pl.BlockSpec((tq, HEAD_W), lambda b, h, i: (b * nq + i, h)),
        out_shape=jax.ShapeDtypeStruct((m, N_HEADS * HEAD_W), BF16),
        scratch_shapes=[pltpu.VMEM((HEAD_W + SUM_ROWS, 2 * tq), F32),
                        pltpu.VMEM((tk, 2 * tq), F32), pltpu.VMEM((tk, 2 * tq), F32)],
        compiler_params=_cparams(("parallel", "parallel", "arbitrary")),
        name="diff_attention",
    )(*args)


def _merge_kernel(x_ref, ao_ref, u_ref, up_ref, un_ref, gates_ref, mod_ref, dw_ref, dwb_ref,
                  lng_ref, lnb_ref, wco_ref, wao_ref, wout_ref, o_ref, ubuf, cbuf, *, tm, tps, rc):
    i = pl.program_id(0)
    first = (i % tps) == 0
    last = (i % tps) == tps - 1
    ubuf[0:CONV_HALO, :] = jnp.where(first, 0.0, up_ref[...])
    ubuf[CONV_HALO:CONV_HALO + tm, :] = u_ref[...]
    ubuf[CONV_HALO + tm:2 * CONV_HALO + tm, :] = jnp.where(last, 0.0, un_ref[...])
    shift = CONV_HALO - CONV_K // 2
    win = rc + 2 * CONV_HALO

    def chunk(r, carry):
        r0 = pl.multiple_of(r * rc, rc)
        for cb in range(D_MODEL // LANES):
            sl = slice(cb * LANES, (cb + 1) * LANES)
            w = ubuf[pl.ds(r0, win), sl]
            acc = jnp.zeros((rc, LANES), F32)
            for rot in range(8):
                wr = w if rot == 0 else pltpu.roll(w, win - rot, 0)
                for k in range(CONV_K):
                    if (k + shift) % 8 == rot:
                        a0 = (k + shift) - rot
                        acc = acc + wr[a0:a0 + rc, :] * dw_ref[k:k + 1, sl]
            cbuf[pl.ds(r0, rc), sl] = acc + dwb_ref[:, sl]
        return carry

    lax.fori_loop(0, tm // rc, chunk, 0)

    cv = cbuf[...]
    mu = jnp.mean(cv, axis=-1, keepdims=True)
    xc = cv - mu
    var = jnp.mean(xc * xc, axis=-1, keepdims=True)
    y = xc * lax.rsqrt(var + EPS) * lng_ref[...] + lnb_ref[...]
    y_conv = jnp.dot(_silu(y).astype(BF16), wco_ref[...], preferred_element_type=F32)
    y_attn = jnp.dot(ao_ref[...], wao_ref[...], preferred_element_type=F32)
    mix = (gates_ref[:, 0:D_MODEL].astype(F32) * y_attn
           + gates_ref[:, D_MODEL:2 * D_MODEL].astype(F32) * y_conv)
    out = jnp.dot(mix.astype(BF16), wout_ref[...], preferred_element_type=F32)
    o_ref[...] = x_ref[...] + mod_ref[:, 2 * D_MODEL:3 * D_MODEL] * out


def _merge(x, ao, u, gates, mod3, row_of, dw, dwb, lng, lnb, wco, wao, wout, seq, tm):
    m, d = x.shape
    tps = seq // tm
    hb = tm // CONV_HALO
    n_halo = m // CONV_HALO
    row = lambda i: (i, 0)
    const = lambda i: (0, 0)
    in_specs = [
        pl.BlockSpec((tm, d), row),
        pl.BlockSpec((tm, d), row),
        pl.BlockSpec((tm, d), row),
        pl.BlockSpec((CONV_HALO, d), lambda i: (jnp.maximum(i * hb - 1, 0), 0)),
        pl.BlockSpec((CONV_HALO, d), lambda i: (jnp.minimum((i + 1) * hb, n_halo - 1), 0)),
        pl.BlockSpec((tm, 2 * d), row),
        pl.BlockSpec((None, 1, mod3.shape[2]), lambda i: (row_of(i), 0, 0)),
        pl.BlockSpec(dw.shape, const),
        pl.BlockSpec((1, d), const), pl.BlockSpec((1, d), const), pl.BlockSpec((1, d), const),
        pl.BlockSpec((d, d), const), pl.BlockSpec((d, d), const), pl.BlockSpec((d, d), const),
    ]
    return pl.pallas_call(
        functools.partial(_merge_kernel, tm=tm, tps=tps, rc=32),
        grid=(m // tm,),
        in_specs=in_specs,
        out_specs=pl.BlockSpec((tm, d), row),
        out_shape=jax.ShapeDtypeStruct((m, d), F32),
        scratch_shapes=[pltpu.VMEM((tm + 2 * CONV_HALO, d), F32), pltpu.VMEM((tm, d), F32)],
        compiler_params=_cparams(("parallel",)),
        name="conv_merge",
    )(x, ao, u, u, u, gates, mod3, dw, dwb, lng, lnb, wco, wao, wout)


def _ffn_kernel(x_ref, mod_ref, g_ref, wg_ref, wu_ref, wd_ref, o_ref, h_ref, acc_ref, *, nf):
    f = pl.program_id(1)

    @pl.when(f == 0)
    def _():
        h = _norm_modulate(x_ref[...], g_ref[...], mod_ref[:, 3 * D_MODEL:4 * D_MODEL],
                           mod_ref[:, 4 * D_MODEL:5 * D_MODEL])
        h_ref[...] = h.astype(BF16)
        acc_ref[...] = jnp.zeros(acc_ref.shape, F32)

    hb = h_ref[...]
    a = jnp.dot(hb, wg_ref[...], preferred_element_type=F32)
    b = jnp.dot(hb, wu_ref[...], preferred_element_type=F32)
    acc_ref[...] += jnp.dot((_silu(a) * b).astype(BF16), wd_ref[...], preferred_element_type=F32)

    @pl.when(f == nf - 1)
    def _():
        o_ref[...] = x_ref[...] + mod_ref[:, 5 * D_MODEL:6 * D_MODEL] * acc_ref[...]


def _dense_ffn(x, mod3, row_of, g, wg, wu, wd, tm, tf):
    m, d = x.shape
    dff = wg.shape[1]
    nf = dff // tf
    return pl.pallas_call(
        functools.partial(_ffn_kernel, nf=nf),
        grid=(m // tm, nf),
        in_specs=[pl.BlockSpec((tm, d), lambda i, f: (i, 0)),
                  pl.BlockSpec((None, 1, mod3.shape[2]), lambda i, f: (row_of(i), 0, 0)),
                  pl.BlockSpec((1, d), lambda i, f: (0, 0)),
                  pl.BlockSpec((d, tf), lambda i, f: (0, f)),
                  pl.BlockSpec((d, tf), lambda i, f: (0, f)),
                  pl.BlockSpec((tf, d), lambda i, f: (f, 0))],
        out_specs=pl.BlockSpec((tm, d), lambda i, f: (i, 0)),
        out_shape=jax.ShapeDtypeStruct((m, d), F32),
        scratch_shapes=[pltpu.VMEM((tm, d), BF16), pltpu.VMEM((tm, d), F32)],
        compiler_params=_cparams(("parallel", "arbitrary")),
        name="dense_ffn",
    )(x, mod3, g, wg, wu, wd)


R_E0, R_E1, R_RANK0, R_RANK1, R_W0, R_W1 = range(6)
ITEM_VALID, ITEM_FIRST, ITEM_LAST = 1, 2, 4


def _router_kernel(x_ref, mod_ref, g_ref, wr_ref, h_ref, route_ref, cum_ref, base_ref):
    @pl.when(pl.program_id(0) == 0)
    def _():
        base_ref[...] = jnp.zeros(base_ref.shape, F32)

    h = _norm_modulate(x_ref[...], g_ref[...], mod_ref[:, 3 * D_MODEL:4 * D_MODEL],
                       mod_ref[:, 4 * D_MODEL:5 * D_MODEL])
    h_ref[...] = h.astype(BF16)
    logits = jnp.dot(h, wr_ref[...], preferred_element_type=F32, precision=lax.Precision.HIGHEST)
    tb = logits.shape[0]
    lane = lax.broadcasted_iota(jnp.int32, logits.shape, 1)
    lg = jnp.where(lane < N_EXPERTS, logits, NEG_BIG)
    m1 = jnp.max(lg, axis=1, keepdims=True)
    e0 = jnp.min(jnp.where(lg == m1, lane, LANES), axis=1, keepdims=True)
    lg2 = jnp.where(lane == e0, NEG_BIG, lg)
    m2 = jnp.max(lg2, axis=1, keepdims=True)
    e1 = jnp.min(jnp.where(lg2 == m2, lane, LANES), axis=1, keepdims=True)
    ex = jnp.exp(m2 - m1)
    w0 = 1.0 / (1.0 + ex)
    w1 = ex / (1.0 + ex)
    oh0 = lane == e0
    oh1 = lane == e1
    oh = jnp.where(oh0, 1.0, 0.0) + jnp.where(oh1, 1.0, 0.0)
    ltri = jnp.where(lax.broadcasted_iota(jnp.int32, (tb, tb), 1)
                     < lax.broadcasted_iota(jnp.int32, (tb, tb), 0), 1.0, 0.0).astype(BF16)
    rank = jnp.dot(ltri, oh.astype(BF16), preferred_element_type=F32) + base_ref[...]
    rank0 = jnp.sum(jnp.where(oh0, rank, 0.0), axis=1, keepdims=True)
    rank1 = jnp.sum(jnp.where(oh1, rank, 0.0), axis=1, keepdims=True)
    route = jnp.zeros(logits.shape, F32)
    for j, val in ((R_E0, e0.astype(F32)), (R_E1, e1.astype(F32)), (R_RANK0, rank0),
                   (R_RANK1, rank1), (R_W0, w0), (R_W1, w1)):
        route = jnp.where(lane == j, val, route)
    route_ref[...] = route
    base = base_ref[...] + jnp.sum(oh, axis=0, keepdims=True)
    base_ref[...] = base
    cum_ref[...] = base


def _router(x, mod3, row_of, g, wr_pad, tb):
    m, d = x.shape
    nb = m // tb
    return pl.pallas_call(
        _router_kernel,
        grid=(nb,),
        in_specs=[pl.BlockSpec((tb, d), lambda i: (i, 0)),
                  pl.BlockSpec((None, 1, mod3.shape[2]), lambda i: (row_of(i), 0, 0)),
                  pl.BlockSpec((1, d), lambda i: (0, 0)),
                  pl.BlockSpec((d, LANES), lambda i: (0, 0))],
        out_specs=[pl.BlockSpec((tb, d), lambda i: (i, 0)),
                   pl.BlockSpec((tb, LANES), lambda i: (i, 0)),
                   pl.BlockSpec((None, 1, LANES), lambda i: (i, 0, 0))],
        out_shape=[jax.ShapeDtypeStruct((m, d), BF16),
                   jax.ShapeDtypeStruct((m, LANES), F32),
                   jax.ShapeDtypeStruct((nb, 1, LANES), F32)],
        scratch_shapes=[pltpu.VMEM((1, LANES), F32)],
        compiler_params=_cparams(("arbitrary",)),
        name="moe_router",
    )(x, mod3, g, wr_pad)


def _moe_plan(route, cum, nb, tb):
    i32 = jnp.int32
    n_tiles = 2 * nb + N_EXPERTS
    n_items = n_tiles + N_EXPERTS * nb
    cum_in = cum[:, 0, :N_EXPERTS].astype(i32)
    cum_ex = jnp.concatenate([jnp.zeros((1, N_EXPERTS), i32), cum_in[:-1]], axis=0)
    counts = cum_in[-1]
    ntile = (counts + tb - 1) // tb
    tile_end = jnp.cumsum(ntile)
    start = (tile_end - ntile) * tb
    n_used = tile_end[-1]
    t_idx = jnp.arange(n_tiles, dtype=i32)
    tile_valid = (t_idx < n_used).astype(i32)
    tile_src = jnp.minimum(t_idx, n_used - 1)
    tile_expert = jnp.minimum(jnp.sum((tile_src[:, None] >= tile_end[None, :]).astype(i32), axis=1),
                              N_EXPERTS - 1)
    e0 = route[:, R_E0].astype(i32)
    e1 = route[:, R_E1].astype(i32)
    pos0 = start[e0] + route[:, R_RANK0].astype(i32)
    pos1 = start[e1] + route[:, R_RANK1].astype(i32)
    lo = start[None, :] + cum_ex
    hi = start[None, :] + cum_in
    t_lo = lo // tb
    npair = jnp.where(hi > lo, (hi - 1) // tb - t_lo + 1, 0)
    j = jnp.arange(2, dtype=i32)
    c_valid = j[None, None, :] < npair[:, :, None]
    c_tile = t_lo[:, :, None] + j[None, None, :]
    c_blk = jnp.broadcast_to(jnp.arange(nb, dtype=i32)[:, None, None], c_valid.shape)

    def compact(perm):
        v = jnp.transpose(c_valid, perm).reshape(-1)
        order = jnp.argsort(jnp.logical_not(v), stable=True)[:n_items]
        n_valid = jnp.sum(v.astype(i32))
        k = jnp.minimum(jnp.arange(n_items, dtype=i32), n_valid - 1)
        take = lambda a: jnp.transpose(a, perm).reshape(-1)[order][k]
        return take(c_tile), take(c_blk), (jnp.arange(n_items, dtype=i32) < n_valid)

    def flags(valid, key):
        prev = jnp.concatenate([key[:1] - 1, key[:-1]])
        nxt = jnp.concatenate([key[1:], key[-1:] - 1])
        last_valid = jnp.concatenate([jnp.logical_not(valid[1:]), jnp.ones((1,), bool)])
        return (valid.astype(i32) * ITEM_VALID
                + (valid & (key != prev)).astype(i32) * ITEM_FIRST
                + (valid & ((key != nxt) | last_valid)).astype(i32) * ITEM_LAST)

    g_tile, g_blk, g_valid = compact((1, 0, 2))
    c_tile2, c_blk2, c_valid2 = compact((0, 1, 2))
    return dict(pos0=pos0, pos1=pos1, start=start, tile_src=tile_src, tile_expert=tile_expert,
                tile_valid=tile_valid, n_tiles=n_tiles, n_items=n_items,
                gather=(g_tile, g_blk, flags(g_valid, g_tile)),
                combine=(c_blk2, c_tile2, flags(c_valid2, c_blk2)))


def _gather_kernel(it_tile, it_blk, it_flag, p0_ref, p1_ref, h_ref, o_ref, *, tb):
    k = pl.program_id(0)
    flag = it_flag[k]

    @pl.when((flag & ITEM_VALID) != 0)
    def _():
        rows = lax.broadcasted_iota(jnp.int32, (tb, tb), 0) + it_tile[k] * tb
        sel = jnp.where(p0_ref[...] == rows, 1.0, 0.0) + jnp.where(p1_ref[...] == rows, 1.0, 0.0)
        c = jnp.dot(sel.astype(BF16), h_ref[...], preferred_element_type=F32)

        @pl.when((flag & ITEM_FIRST) != 0)
        def _():
            o_ref[...] = c.astype(BF16)

        @pl.when((flag & ITEM_FIRST) == 0)
        def _():
            o_ref[...] = (o_ref[...].astype(F32) + c).astype(BF16)


def _moe_gather(h, plan, tb):
    m, d = h.shape
    nb = m // tb
    it_tile, it_blk, it_flag = plan["gather"]
    p0 = plan["pos0"].reshape(nb, 1, tb)
    p1 = plan["pos1"].reshape(nb, 1, tb)
    grid_spec = pltpu.PrefetchScalarGridSpec(
        num_scalar_prefetch=3,
        grid=(plan["n_items"],),
        in_specs=[pl.BlockSpec((None, 1, tb), lambda k, t, b, f: (b[k], 0, 0)),
                  pl.BlockSpec((None, 1, tb), lambda k, t, b, f: (b[k], 0, 0)),
                  pl.BlockSpec((tb, d), lambda k, t, b, f: (b[k], 0))],
        out_specs=pl.BlockSpec((tb, d), lambda k, t, b, f: (t[k], 0)),
    )
    return pl.pallas_call(
        functools.partial(_gather_kernel, tb=tb),
        grid_spec=grid_spec,
        out_shape=jax.ShapeDtypeStruct((plan["n_tiles"] * tb, d), BF16),
        compiler_params=_cparams(("arbitrary",)),
        name="moe_gather",
    )(it_tile, it_blk, it_flag, p0, p1, h)


def _expert_kernel(ts_ref, te_ref, tv_ref, x_ref, wg_ref, wu_ref, wd_ref, o_ref, acc_ref, *, nf):
    t = pl.program_id(0)
    f = pl.program_id(1)

    @pl.when(tv_ref[t] != 0)
    def _():
        xb = x_ref[...]
        a = jnp.dot(xb, wg_ref[...], preferred_element_type=F32)
        b = jnp.dot(xb, wu_ref[...], preferred_element_type=F32)
        y = jnp.dot((_silu(a) * b).astype(BF16), wd_ref[...], preferred_element_type=F32)

        @pl.when(f == 0)
        def _():
            acc_ref[...] = y

        @pl.when(f != 0)
        def _():
            acc_ref[...] += y

        @pl.when(f == nf - 1)
        def _():
            o_ref[...] = acc_ref[...].astype(BF16)


def _moe_experts(xs, plan, wg, wu, wd, tb, tf):
    d = xs.shape[1]
    nf = wg.shape[2] // tf
    fsel = lambda f, tv, t: jnp.where(tv[t] != 0, f, nf - 1)
    grid_spec = pltpu.PrefetchScalarGridSpec(
        num_scalar_prefetch=3,
        grid=(plan["n_tiles"], nf),
        in_specs=[pl.BlockSpec((tb, d), lambda t, f, ts, te, tv: (ts[t], 0)),
                  pl.BlockSpec((None, d, tf), lambda t, f, ts, te, tv: (te[t], 0, fsel(f, tv, t))),
                  pl.BlockSpec((None, d, tf), lambda t, f, ts, te, tv: (te[t], 0, fsel(f, tv, t))),
                  pl.BlockSpec((None, tf, d), lambda t, f, ts, te, tv: (te[t], fsel(f, tv, t), 0))],
        out_specs=pl.BlockSpec((tb, d), lambda t, f, ts, te, tv: (ts[t], 0)),
        scratch_shapes=[pltpu.VMEM((tb, d), F32)],
    )
    return pl.pallas_call(
        functools.partial(_expert_kernel, nf=nf),
        grid_spec=grid_spec,
        out_shape=jax.ShapeDtypeStruct(xs.shape, BF16),
        compiler_params=_cparams(("arbitrary", "arbitrary")),
        name="moe_experts",
    )(plan["tile_src"], plan["tile_expert"], plan["tile_valid"], xs, wg, wu, wd)


def _combine_kernel(it_blk, it_tile, it_flag, x_ref, mod_ref, route_ref, start_ref, y_ref,
                    o_ref, acc_ref, *, tb):
    k = pl.program_id(0)
    flag = it_flag[k]

    @pl.when((flag & ITEM_VALID) != 0)
    def _():
        route = route_ref[...]
        lanef = lax.broadcasted_iota(jnp.int32, route.shape, 1).astype(F32)
        start = start_ref[...]
        base = (it_tile[k] * tb).astype(F32)

        def slot(je, jr, jw):
            e = route[:, je:je + 1]
            s = jnp.sum(jnp.where(lanef == e, start, 0.0), axis=1, keepdims=True)
            return s + route[:, jr:jr + 1] - base, route[:, jw:jw + 1]

        pos0, w0 = slot(R_E0, R_RANK0, R_W0)
        pos1, w1 = slot(R_E1, R_RANK1, R_W1)
        col = lax.broadcasted_iota(jnp.int32, (tb, tb), 1).astype(F32)
        sel = jnp.where(col == pos0, w0, 0.0) + jnp.where(col == pos1, w1, 0.0)
        c = jnp.dot(sel.astype(BF16), y_ref[...], preferred_element_type=F32)

        @pl.when((flag & ITEM_FIRST) != 0)
        def _():
            acc_ref[...] = c

        @pl.when((flag & ITEM_FIRST) == 0)
        def _():
            acc_ref[...] += c

        @pl.when((flag & ITEM_LAST) != 0)
        def _():
            o_ref[...] = x_ref[...] + mod_ref[:, 5 * D_MODEL:6 * D_MODEL] * acc_ref[...]


def _moe_combine(x, mod3, row_of, route, ys, plan, tb):
    m, d = x.shape
    it_blk, it_tile, it_flag = plan["combine"]
    start = jnp.zeros((1, LANES), F32).at[0, :N_EXPERTS].set(plan["start"].astype(F32))
    grid_spec = pltpu.PrefetchScalarGridSpec(
        num_scalar_prefetch=3,
        grid=(plan["n_items"],),
        in_specs=[pl.BlockSpec((tb, d), lambda k, b, t, f: (b[k], 0)),
                  pl.BlockSpec((None, 1, mod3.shape[2]), lambda k, b, t, f: (row_of(b[k]), 0, 0)),
                  pl.BlockSpec((tb, LANES), lambda k, b, t, f: (b[k], 0)),
                  pl.BlockSpec((1, LANES), lambda k, b, t, f: (0, 0)),
                  pl.BlockSpec((tb, d), lambda k, b, t, f: (t[k], 0))],
        out_specs=pl.BlockSpec((tb, d), lambda k, b, t, f: (b[k], 0)),
        scratch_shapes=[pltpu.VMEM((tb, d), F32)],
    )
    return pl.pallas_call(
        functools.partial(_combine_kernel, tb=tb),
        grid_spec=grid_spec,
        out_shape=jax.ShapeDtypeStruct((m, d), F32),
        compiler_params=_cparams(("arbitrary",)),
        name="moe_combine",
    )(it_blk, it_tile, it_flag, x, mod3, route, start, ys)


def _moe_ffn(x, mod3, row_of, g, wr_pad, wg, wu, wd, tb, tf):
    nb = x.shape[0] // tb
    h, route, cum = _router(x, mod3, row_of, g, wr_pad, tb)
    plan = _moe_plan(route, cum, nb, tb)
    xs = _moe_gather(h, plan, tb)
    ys = _moe_experts(xs, plan, wg, wu, wd, tb, tf)
    return _moe_combine(x, mod3, row_of, route, ys, plan, tb)


def _rope_tables(seq):
    t = jnp.arange(seq, dtype=jnp.int32)
    row = (t // GRID_W).astype(F32)
    col = (t % GRID_W).astype(F32)
    inv = ROPE_THETA ** (-jnp.arange(0, ROT_AXIS, 2, dtype=F32) / ROT_AXIS)
    ar = row[:, None] * inv
    ac = col[:, None] * inv
    ang = jnp.concatenate([ar, ar, ac, ac], axis=-1)
    ang = jnp.concatenate([ang, ang], axis=-1)
    cos, sin = jnp.cos(ang), jnp.sin(ang)
    upper = (jnp.arange(HEAD_W) & ROT_HALF) != 0
    sa = jnp.where(upper, 0.0, -sin)
    sb = jnp.where(upper, sin, 0.0)
    return cos, sa, sb


def _pick_tile(n, pref):
    t = min(n, pref)
    while n % t:
        t //= 2
    return t


def kernel(x, c, ctx, c_ctx, w_mod, b_mod, g_mix, w_in, q_norm_g, k_norm_g, lambda_q1, lambda_k1, lambda_q2, lambda_k2, subln_g, w_attn_o, dw_weight, dw_bias, conv_ln_g, conv_ln_b, w_conv_o, w_out, g_ffn, w_ff_gate, w_ff_up, w_ff_down, w_router, w_exp_gate, w_exp_up, w_exp_down):
    n_batch, seq, d = x.shape
    ctx_len = ctx.shape[1]
    depth = w_mod.shape[0]
    assert d == D_MODEL and seq % GRID_W == 0

    tm_l = _pick_tile(seq, 512)
    tm_c = _pick_tile(ctx_len, 512)
    ctx_row = n_batch
    mod_rows = -(-(n_batch + 1) // 8) * 8
    cvec = jnp.zeros((mod_rows, d), F32).at[:n_batch].set(c).at[ctx_row].set(c_ctx)

    cos, sa, sb = _rope_tables(seq)
    tabs = (cos * Q_SCALE, sa * Q_SCALE, sb * Q_SCALE, cos, sa, sb)
    gid = jnp.arange(HEAD_W) // HEAD_DIM
    gsum = (gid[:, None] == gid[None, :]).astype(BF16)

    xl = x.reshape(n_batch * seq, d)
    xc = ctx.reshape(n_batch * ctx_len, d)
    lat_row = lambda i: (i * tm_l) // seq
    ctx_row_of = lambda i: ctx_row

    for l in range(depth):
        last = l == depth - 1
        lam_init = 0.8 - 0.6 * math.exp(-0.3 * l)
        mod3 = _modulation(cvec, w_mod[l], b_mod[l]).reshape(mod_rows, 1, 6 * d)
        w_bf = w_in[l].astype(BF16)
        gmix = g_mix[l].reshape(1, d)
        qg = jnp.tile(q_norm_g[l], 2).reshape(1, HEAD_W)
        kg = jnp.tile(k_norm_g[l], 2).reshape(1, HEAD_W)
        lamv = jnp.stack([lambda_q1[l], lambda_k1[l], lambda_q2[l], lambda_k2[l]])
        sg = subln_g[l].reshape(HEAD_W, 1)
        dw = jnp.zeros((32, d), F32).at[:CONV_K].set(dw_weight[l])
        merge_w = (dw, dw_bias[l].reshape(1, d), conv_ln_g[l].reshape(1, d),
                   conv_ln_b[l].reshape(1, d), w_conv_o[l].astype(BF16),
                   w_attn_o[l].astype(BF16), w_out[l].astype(BF16))

        q, k, v, u, gates = _in_proj(xl, mod3, lat_row, gmix, w_bf, qg, kg, gsum, tabs,
                                     seq, 0, N_GROUPS, tm_l)
        if last:
            kc, vc = _in_proj(xc, mod3, ctx_row_of, gmix, w_bf, qg, kg, gsum, None,
                              ctx_len, 1, 2, tm_c)
        else:
            qc, kc, vc, uc, gatesc = _in_proj(xc, mod3, ctx_row_of, gmix, w_bf, qg, kg, gsum,
                                              None, ctx_len, 0, N_GROUPS, tm_c)
        tk = math.gcd(math.gcd(seq, ctx_len), ATTN_TK)
        ao = _diff_attention(q, [(k, v), (kc, vc)], lamv, sg, n_batch, seq,
                             _pick_tile(seq, ATTN_TQ), tk, lam_init)
        xl = _merge(xl, ao, u, gates, mod3, lat_row, *merge_w, seq, tm_l)
        if not last:
            aoc = _diff_attention(qc, [(kc, vc)], lamv, sg, n_batch, ctx_len,
                                  _pick_tile(ctx_len, ATTN_TQ), tk, lam_init)
            xc = _merge(xc, aoc, uc, gatesc, mod3, ctx_row_of, *merge_w, ctx_len, tm_c)

        gf = g_ffn[l].reshape(1, d)
        i = l // 2
        if l % 2 == 0:
            wts = (w_ff_gate[i].astype(BF16), w_ff_up[i].astype(BF16), w_ff_down[i].astype(BF16))
            tf = _pick_ff_tile(wts[0].shape[1])
            xl = _dense_ffn(xl, mod3, lat_row, gf, *wts, tm_l, tf)
            if not last:
                xc = _dense_ffn(xc, mod3, ctx_row_of, gf, *wts, tm_c, tf)
        else:
            wr_pad = jnp.zeros((d, LANES), F32).at[:, :N_EXPERTS].set(w_router[i])
            wts = (w_exp_gate[i].astype(BF16), w_exp_up[i].astype(BF16),
                   w_exp_down[i].astype(BF16))
            tf = _pick_ff_tile(wts[0].shape[2])
            xl = _moe_ffn(xl, mod3, lat_row, gf, wr_pad, *wts, tm_l, tf)
            if not last:
                xc = _moe_ffn(xc, mod3, ctx_row_of, gf, wr_pad, *wts, tm_c, tf)
    return xl.reshape(n_batch, seq, d)


def _pick_ff_tile(dff):
    best = LANES
    for t in range(LANES, 1792 + 1, LANES):
        if dff % t == 0:
            best = t
    return best
```

```python
import functools
import math

import jax
import jax.numpy as jnp
from jax import lax
from jax.experimental import pallas as pl
from jax.experimental.pallas import tpu as pltpu

F32 = jnp.float32
BF16 = jnp.bfloat16

LANES = 128
D_MODEL = 1024
N_HEADS = 8
HEAD_DIM = 64
HEAD_W = 2 * HEAD_DIM
GRID_W = 64
ROT_AXIS = HEAD_DIM // 2
ROT_HALF = ROT_AXIS // 2
ROPE_THETA = 10000.0
CONV_K = 31
CONV_HALO = 16
N_EXPERTS = 8
EPS = 1e-6
N_GROUPS = 7
ATTN_TQ = 256
ATTN_TK = 512
MOE_WINDOW = 256
KNORM_ROWS = 512
BOUND_SLACK = 1.01
MIN_DENOM = 2.0 ** -60
VMEM_LIMIT = 52 * 1024 * 1024
Q_SCALE = (HEAD_DIM ** -0.5) * math.log2(math.e)
NEG_BIG = -1e30


def _cparams(sem):
    return pltpu.CompilerParams(dimension_semantics=sem, vmem_limit_bytes=VMEM_LIMIT)


def _sigmoid(v):
    return 0.5 * jnp.tanh(0.5 * v) + 0.5


def _silu(v):
    return v * _sigmoid(v)


def _norm_modulate(xf, g, shift, scale):
    ms = jnp.mean(xf * xf, axis=-1, keepdims=True)
    return (xf * lax.rsqrt(ms + EPS) * g) * (1.0 + scale) + shift


def _mod_kernel(c_ref, w_ref, b_ref, o_ref):
    a = _silu(c_ref[...])
    o_ref[...] = jnp.dot(a, w_ref[...], preferred_element_type=F32,
                         precision=lax.Precision.HIGHEST) + b_ref[...]


def _modulation(cvec, w, b):
    rows, d = cvec.shape
    n = w.shape[1]
    tn = n // 4
    return pl.pallas_call(
        _mod_kernel,
        grid=(n // tn,),
        in_specs=[pl.BlockSpec((rows, d), lambda j: (0, 0)),
                  pl.BlockSpec((d, tn), lambda j: (0, j)),
                  pl.BlockSpec((1, tn), lambda j: (0, j))],
        out_specs=pl.BlockSpec((rows, tn), lambda j: (0, j)),
        out_shape=jax.ShapeDtypeStruct((rows, n), F32),
        compiler_params=_cparams(("arbitrary",)),
        name="modulation",
    )(cvec, w, b.reshape(1, n))


def _in_kernel(*refs, g0, ng, rope):
    x_ref, mod_ref, gmix_ref, w_ref, qg_ref, kg_ref, gsum_ref = refs[:7]
    pos = 7
    if rope:
        tq_refs = refs[pos:pos + 3]
        tk_refs = refs[pos + 3:pos + 6]
        pos += 6
    full = ng == N_GROUPS
    if full:
        q_ref, k_ref, v_ref, u_ref, gates_ref = refs[pos:pos + 5]
        pos += 5
    else:
        k_ref, v_ref = refs[pos:pos + 2]
        pos += 2
    h_ref = refs[pos]
    j = pl.program_id(1) + g0

    @pl.when(pl.program_id(1) == 0)
    def _():
        h = _norm_modulate(x_ref[...], gmix_ref[...], mod_ref[:, 0:D_MODEL],
                           mod_ref[:, D_MODEL:2 * D_MODEL])
        h_ref[...] = h.astype(BF16)

    acc = jnp.dot(h_ref[...], w_ref[...], preferred_element_type=F32)

    def qk_norm(out_ref, g_row_ref, tables):
        gsum = gsum_ref[...]
        for hb in range(N_HEADS):
            sl = slice(hb * HEAD_W, (hb + 1) * HEAD_W)
            blk = acc[:, sl]
            ms = jnp.dot((blk * blk).astype(BF16), gsum, preferred_element_type=F32)
            y = blk * lax.rsqrt(ms + EPS)
            if tables is not None:
                c_ref, sa_ref, sb_ref = tables
                y = (y * c_ref[...] + pltpu.roll(y, HEAD_W - ROT_HALF, 1) * sa_ref[...]
                     + pltpu.roll(y, ROT_HALF, 1) * sb_ref[...])
            else:
                y = y * g_row_ref[...]
            out_ref[:, sl] = y.astype(BF16)

    if full:
        @pl.when(j == 0)
        def _():
            qk_norm(q_ref, qg_ref, tq_refs if rope else None)

    @pl.when(j == 1)
    def _():
        qk_norm(k_ref, kg_ref, tk_refs if rope else None)

    @pl.when(j == 2)
    def _():
        v_ref[...] = acc.T.astype(BF16)

    if full:
        @pl.when(j == 3)
        def _():
            u_ref[...] = acc

        @pl.when(j == 4)
        def _():
            u_ref[...] = u_ref[...] * _sigmoid(acc)

        @pl.when(j == 5)
        def _():
            gates_ref[:, 0:D_MODEL] = _sigmoid(acc).astype(BF16)

        @pl.when(j == 6)
        def _():
            gates_ref[:, D_MODEL:2 * D_MODEL] = _sigmoid(acc).astype(BF16)


def _in_proj(x, mod3, row_of, gmix, w_bf, qg, kg, gsum, rope_tabs, seq, g0, ng, tm):
    m, d = x.shape
    tps = seq // tm
    rope = rope_tabs is not None
    full = ng == N_GROUPS
    in_specs = [
        pl.BlockSpec((tm, d), lambda i, j: (i, 0)),
        pl.BlockSpec((None, 1, mod3.shape[2]), lambda i, j: (row_of(i), 0, 0)),
        pl.BlockSpec((1, d), lambda i, j: (0, 0)),
        pl.BlockSpec((d, D_MODEL), lambda i, j: (0, j + g0)),
        pl.BlockSpec((1, HEAD_W), lambda i, j: (0, 0)),
        pl.BlockSpec((1, HEAD_W), lambda i, j: (0, 0)),
        pl.BlockSpec((HEAD_W, HEAD_W), lambda i, j: (0, 0)),
    ]
    args = [x, mod3, gmix, w_bf, qg, kg, gsum]
    if rope:
        for t in rope_tabs:
            in_specs.append(pl.BlockSpec((tm, HEAD_W), lambda i, j: (i % tps, 0)))
            args.append(t)
    row_spec = pl.BlockSpec((tm, D_MODEL), lambda i, j: (i, 0))
    rows_bf = jax.ShapeDtypeStruct((m, D_MODEL), BF16)
    vt_spec = pl.BlockSpec((D_MODEL, tm), lambda i, j: (i // tps, i % tps))
    vt_shape = jax.ShapeDtypeStruct(((m // seq) * D_MODEL, seq), BF16)
    if full:
        out_specs = [row_spec, row_spec, vt_spec, row_spec,
                     pl.BlockSpec((tm, 2 * D_MODEL), lambda i, j: (i, 0))]
        out_shape = [rows_bf, rows_bf, vt_shape, jax.ShapeDtypeStruct((m, D_MODEL), F32),
                     jax.ShapeDtypeStruct((m, 2 * D_MODEL), BF16)]
    else:
        out_specs = [row_spec, vt_spec]
        out_shape = [rows_bf, vt_shape]
    return pl.pallas_call(
        functools.partial(_in_kernel, g0=g0, ng=ng, rope=rope),
        grid=(m // tm, ng),
        in_specs=in_specs,
        out_specs=out_specs,
        out_shape=out_shape,
        scratch_shapes=[pltpu.VMEM((tm, d), BF16)],
        compiler_params=_cparams(("parallel", "arbitrary")),
        name="in_proj",
    )(*args)


def _attn_kernel(*refs, src_lens, tq, tk, lam_init):
    n_src = len(src_lens)
    q_ref = refs[0]
    kv_refs = refs[1:1 + 2 * n_src]
    lamv_ref, sg_ref, ksum_ref, o_ref, acc_ref, kmax_ref = refs[1 + 2 * n_src:7 + 2 * n_src]
    s_refs = refs[7 + 2 * n_src:9 + 2 * n_src]
    n_cb = 2 * tq // LANES

    @pl.when(pl.program_id(2) == 0)
    def _():
        best = jnp.zeros((1, HEAD_W), F32)
        for si in range(n_src):
            for off in range(0, src_lens[si], KNORM_ROWS):
                rows = min(KNORM_ROWS, src_lens[si] - off)
                kf = kv_refs[2 * si][off:off + rows, :].astype(F32)
                n2 = jnp.dot((kf * kf).astype(BF16), ksum_ref[...], preferred_element_type=F32)
                best = jnp.maximum(best, jnp.max(n2, axis=0, keepdims=True))
        kmax_ref[...] = jnp.sqrt(best) * BOUND_SLACK

    q = q_ref[...]
    lane = lax.broadcasted_iota(jnp.int32, q.shape, 1)
    zero = jnp.zeros_like(q)
    q2 = jnp.concatenate([jnp.where(lane < HEAD_DIM, q, zero),
                          jnp.where(lane >= HEAD_DIM, q, zero)], axis=0)
    chunks = [(si, off, min(tk, src_lens[si])) for si in range(n_src)
              for off in range(0, src_lens[si], min(tk, src_lens[si]))]

    def scores(si, off, tkk):
        kc = kv_refs[2 * si][pl.ds(off, tkk), :]
        return lax.dot_general(kc, q2, (((1,), (1,)), ((), ())), preferred_element_type=F32)

    def finalize(acc, l):
        lv = lamv_ref[...]
        lam = (jnp.exp(jnp.sum(lv[0:1] * lv[1:2], axis=1, keepdims=True))
               - jnp.exp(jnp.sum(lv[2:3] * lv[3:4], axis=1, keepdims=True)) + lam_init)
        inv = 1.0 / l
        o_t = acc[:, 0:tq] * inv[:, 0:tq] - lam * (acc[:, tq:2 * tq] * inv[:, tq:2 * tq])
        ms = jnp.mean(o_t * o_t, axis=0, keepdims=True)
        y_t = o_t * lax.rsqrt(ms + EPS) * (sg_ref[...] * (1.0 - lam_init))
        o_ref[...] = y_t.T.astype(BF16)

    q2f = q2.astype(F32)
    qn2 = lax.dot_general(jnp.ones((8, HEAD_W), BF16), (q2f * q2f).astype(BF16),
                          (((1,), (1,)), ((), ())), preferred_element_type=F32)[0:1, :]
    kmax = kmax_ref[...]
    kmax_row = jnp.concatenate([jnp.broadcast_to(kmax[:, 0:1], (1, tq)),
                                jnp.broadcast_to(kmax[:, HEAD_DIM:HEAD_DIM + 1], (1, tq))], axis=1)
    bound = jnp.sqrt(qn2) * BOUND_SLACK * kmax_row
    s_refs[0][0:chunks[0][2], :] = scores(*chunks[0])
    l = jnp.zeros((1, 2 * tq), F32)
    for ci, (si, off, tkk) in enumerate(chunks):
        if ci + 1 < len(chunks):
            s_refs[(ci + 1) % 2][0:chunks[ci + 1][2], :] = scores(*chunks[ci + 1])
        p = jnp.exp2(s_refs[ci % 2][0:tkk, :] - bound)
        l = l + jnp.sum(p, axis=0, keepdims=True)
        pv = jnp.dot(kv_refs[2 * si + 1][:, pl.ds(off, tkk)], p.astype(BF16),
                     preferred_element_type=F32)
        if ci == 0:
            acc_ref[...] = pv
        else:
            acc_ref[...] += pv
    healthy = jnp.min(l) >= MIN_DENOM

    @pl.when(healthy)
    def _():
        finalize(acc_ref[...], l)

    @pl.when(jnp.logical_not(healthy))
    def _():
        acc_ref[...] = jnp.zeros(acc_ref.shape, F32)
        m = [jnp.full((1, LANES), NEG_BIG, F32) for _ in range(n_cb)]
        lx = [jnp.zeros((1, LANES), F32) for _ in range(n_cb)]
        for si, off, tkk in chunks:
            s = scores(si, off, tkk)
            alphas, ps = [], []
            for cb in range(n_cb):
                s_cb = s[:, cb * LANES:(cb + 1) * LANES]
                m_new = jnp.maximum(m[cb], jnp.max(s_cb, axis=0, keepdims=True))
                alpha = jnp.exp2(m[cb] - m_new)
                p_cb = jnp.exp2(s_cb - m_new)
                lx[cb] = alpha * lx[cb] + jnp.sum(p_cb, axis=0, keepdims=True)
                alphas.append(alpha)
                ps.append(p_cb.astype(BF16))
                m[cb] = m_new
            acc_ref[...] = (jnp.concatenate(alphas, axis=1) * acc_ref[...]
                            + jnp.dot(kv_refs[2 * si + 1][:, pl.ds(off, tkk)],
                                      jnp.concatenate(ps, axis=1), preferred_element_type=F32))
        finalize(acc_ref[...], jnp.concatenate(lx, axis=1))


def _diff_attention(q, kvs, lamv, sg_col, n_batch, sq, tq, tk, lam_init):
    m = q.shape[0]
    nq = sq // tq
    in_specs = [pl.BlockSpec((tq, HEAD_W), lambda b, h, i: (b * nq + i, h))]
    args = [q]
    src_lens = []
    for (k, vt) in kvs:
        sl = k.shape[0] // n_batch
        src_lens.append(sl)
        in_specs.append(pl.BlockSpec((sl, HEAD_W), lambda b, h, i: (b, h)))
        in_specs.append(pl.BlockSpec((HEAD_W, sl), lambda b, h, i: (b * N_HEADS + h, 0)))
        args += [k, vt]
    in_specs += [pl.BlockSpec(lamv.shape, lambda b, h, i: (0, 0)),
                 pl.BlockSpec((HEAD_W, 1), lambda b, h, i: (0, 0)),
                 pl.BlockSpec((HEAD_W, HEAD_W), lambda b, h, i: (0, 0))]
    gid = jnp.arange(HEAD_W) // HEAD_DIM
    args += [lamv, sg_col, (gid[:, None] == gid[None, :]).astype(BF16)]
    return pl.pallas_call(
        functools.partial(_attn_kernel, src_lens=tuple(src_lens), tq=tq, tk=tk,
                          lam_init=lam_init),
        grid=(n_batch, N_HEADS, nq),
        in_specs=in_specs,
        out_specs=pl.BlockSpec((tq, HEAD_W), lambda b, h, i: (b * nq + i, h)),
        out_shape=jax.ShapeDtypeStruct((m, N_HEADS * HEAD_W), BF16),
        scratch_shapes=[pltpu.VMEM((HEAD_W, 2 * tq), F32), pltpu.VMEM((1, HEAD_W), F32),
                        pltpu.VMEM((tk, 2 * tq), F32), pltpu.VMEM((tk, 2 * tq), F32)],
        compiler_params=_cparams(("parallel", "parallel", "arbitrary")),
        name="diff_attention",
    )(*args)


def _merge_kernel(x_ref, ao_ref, u_ref, up_ref, un_ref, gates_ref, mod_ref, dw_ref, dwb_ref,
                  lng_ref, lnb_ref, wco_ref, wao_ref, wout_ref, o_ref, ubuf, cbuf, *, tm, tps, rc):
    i = pl.program_id(0)
    first = (i % tps) == 0
    last = (i % tps) == tps - 1
    ubuf[0:CONV_HALO, :] = jnp.where(first, 0.0, up_ref[...])
    ubuf[CONV_HALO:CONV_HALO + tm, :] = u_ref[...]
    ubuf[CONV_HALO + tm:2 * CONV_HALO + tm, :] = jnp.where(last, 0.0, un_ref[...])
    shift = CONV_HALO - CONV_K // 2
    win = rc + 2 * CONV_HALO

    def chunk(r, carry):
        r0 = pl.multiple_of(r * rc, rc)
        for cb in range(D_MODEL // LANES):
            sl = slice(cb * LANES, (cb + 1) * LANES)
            w = ubuf[pl.ds(r0, win), sl]
            acc = jnp.zeros((rc, LANES), F32)
            for rot in range(8):
                wr = w if rot == 0 else pltpu.roll(w, win - rot, 0)
                for k in range(CONV_K):
                    if (k + shift) % 8 == rot:
                        a0 = (k + shift) - rot
                        acc = acc + wr[a0:a0 + rc, :] * dw_ref[k:k + 1, sl]
            cbuf[pl.ds(r0, rc), sl] = acc + dwb_ref[:, sl]
        return carry

    lax.fori_loop(0, tm // rc, chunk, 0)

    cv = cbuf[...]
    mu = jnp.mean(cv, axis=-1, keepdims=True)
    xc = cv - mu
    var = jnp.mean(xc * xc, axis=-1, keepdims=True)
    y = xc * lax.rsqrt(var + EPS) * lng_ref[...] + lnb_ref[...]
    y_conv = jnp.dot(_silu(y).astype(BF16), wco_ref[...], preferred_element_type=F32)
    y_attn = jnp.dot(ao_ref[...], wao_ref[...], preferred_element_type=F32)
    mix = (gates_ref[:, 0:D_MODEL].astype(F32) * y_attn
           + gates_ref[:, D_MODEL:2 * D_MODEL].astype(F32) * y_conv)
    out = jnp.dot(mix.astype(BF16), wout_ref[...], preferred_element_type=F32)
    o_ref[...] = x_ref[...] + mod_ref[:, 2 * D_MODEL:3 * D_MODEL] * out


def _merge(x, ao, u, gates, mod3, row_of, dw, dwb, lng, lnb, wco, wao, wout, seq, tm):
    m, d = x.shape
    tps = seq // tm
    hb = tm // CONV_HALO
    n_halo = m // CONV_HALO
    row = lambda i: (i, 0)
    const = lambda i: (0, 0)
    in_specs = [
        pl.BlockSpec((tm, d), row),
        pl.BlockSpec((tm, d), row),
        pl.BlockSpec((tm, d), row),
        pl.BlockSpec((CONV_HALO, d), lambda i: (jnp.maximum(i * hb - 1, 0), 0)),
        pl.BlockSpec((CONV_HALO, d), lambda i: (jnp.minimum((i + 1) * hb, n_halo - 1), 0)),
        pl.BlockSpec((tm, 2 * d), row),
        pl.BlockSpec((None, 1, mod3.shape[2]), lambda i: (row_of(i), 0, 0)),
        pl.BlockSpec(dw.shape, const),
        pl.BlockSpec((1, d), const), pl.BlockSpec((1, d), const), pl.BlockSpec((1, d), const),
        pl.BlockSpec((d, d), const), pl.BlockSpec((d, d), const), pl.BlockSpec((d, d), const),
    ]
    return pl.pallas_call(
        functools.partial(_merge_kernel, tm=tm, tps=tps, rc=32),
        grid=(m // tm,),
        in_specs=in_specs,
        out_specs=pl.BlockSpec((tm, d), row),
        out_shape=jax.ShapeDtypeStruct((m, d), F32),
        scratch_shapes=[pltpu.VMEM((tm + 2 * CONV_HALO, d), F32), pltpu.VMEM((tm, d), F32)],
        compiler_params=_cparams(("parallel",)),
        name="conv_merge",
    )(x, ao, u, u, u, gates, mod3, dw, dwb, lng, lnb, wco, wao, wout)


def _ffn_kernel(x_ref, mod_ref, g_ref, wg_ref, wu_ref, wd_ref, o_ref, h_ref, acc_ref, *, nf):
    f = pl.program_id(1)

    @pl.when(f == 0)
    def _():
        h = _norm_modulate(x_ref[...], g_ref[...], mod_ref[:, 3 * D_MODEL:4 * D_MODEL],
                           mod_ref[:, 4 * D_MODEL:5 * D_MODEL])
        h_ref[...] = h.astype(BF16)
        acc_ref[...] = jnp.zeros(acc_ref.shape, F32)

    hb = h_ref[...]
    a = jnp.dot(hb, wg_ref[...], preferred_element_type=F32)
    b = jnp.dot(hb, wu_ref[...], preferred_element_type=F32)
    acc_ref[...] += jnp.dot((_silu(a) * b).astype(BF16), wd_ref[...], preferred_element_type=F32)

    @pl.when(f == nf - 1)
    def _():
        o_ref[...] = x_ref[...] + mod_ref[:, 5 * D_MODEL:6 * D_MODEL] * acc_ref[...]


def _dense_ffn(x, mod3, row_of, g, wg, wu, wd, tm, tf):
    m, d = x.shape
    dff = wg.shape[1]
    nf = dff // tf
    return pl.pallas_call(
        functools.partial(_ffn_kernel, nf=nf),
        grid=(m // tm, nf),
        in_specs=[pl.BlockSpec((tm, d), lambda i, f: (i, 0)),
                  pl.BlockSpec((None, 1, mod3.shape[2]), lambda i, f: (row_of(i), 0, 0)),
                  pl.BlockSpec((1, d), lambda i, f: (0, 0)),
                  pl.BlockSpec((d, tf), lambda i, f: (0, f)),
                  pl.BlockSpec((d, tf), lambda i, f: (0, f)),
                  pl.BlockSpec((tf, d), lambda i, f: (f, 0))],
        out_specs=pl.BlockSpec((tm, d), lambda i, f: (i, 0)),
        out_shape=jax.ShapeDtypeStruct((m, d), F32),
        scratch_shapes=[pltpu.VMEM((tm, d), BF16), pltpu.VMEM((tm, d), F32)],
        compiler_params=_cparams(("parallel", "arbitrary")),
        name="dense_ffn",
    )(x, mod3, g, wg, wu, wd)


R_E0, R_E1, R_RANK0, R_RANK1, R_W0, R_W1 = range(6)
ITEM_VALID, ITEM_FIRST, ITEM_LAST, ITEM_ZERO = 1, 2, 4, 8


def _router_kernel(x_ref, mod_ref, g_ref, wr_ref, h_ref, route_ref, cum_ref, base_ref):
    @pl.when(pl.program_id(0) == 0)
    def _():
        base_ref[...] = jnp.zeros(base_ref.shape, F32)

    h = _norm_modulate(x_ref[...], g_ref[...], mod_ref[:, 3 * D_MODEL:4 * D_MODEL],
                       mod_ref[:, 4 * D_MODEL:5 * D_MODEL])
    h_ref[...] = h.astype(BF16)
    logits = jnp.dot(h, wr_ref[...], preferred_element_type=F32, precision=lax.Precision.HIGHEST)
    tb = logits.shape[0]
    lane = lax.broadcasted_iota(jnp.int32, logits.shape, 1)
    lg = jnp.where(lane < N_EXPERTS, logits, NEG_BIG)
    m1 = jnp.max(lg, axis=1, keepdims=True)
    e0 = jnp.min(jnp.where(lg == m1, lane, LANES), axis=1, keepdims=True)
    lg2 = jnp.where(lane == e0, NEG_BIG, lg)
    m2 = jnp.max(lg2, axis=1, keepdims=True)
    e1 = jnp.min(jnp.where(lg2 == m2, lane, LANES), axis=1, keepdims=True)
    ex = jnp.exp(m2 - m1)
    w0 = 1.0 / (1.0 + ex)
    w1 = ex / (1.0 + ex)
    oh0 = lane == e0
    oh1 = lane == e1
    oh = jnp.where(oh0, 1.0, 0.0) + jnp.where(oh1, 1.0, 0.0)
    ltri = jnp.where(lax.broadcasted_iota(jnp.int32, (tb, tb), 1)
                     < lax.broadcasted_iota(jnp.int32, (tb, tb), 0), 1.0, 0.0).astype(BF16)
    rank = jnp.dot(ltri, oh.astype(BF16), preferred_element_type=F32) + base_ref[...]
    rank0 = jnp.sum(jnp.where(oh0, rank, 0.0), axis=1, keepdims=True)
    rank1 = jnp.sum(jnp.where(oh1, rank, 0.0), axis=1, keepdims=True)
    route = jnp.zeros(logits.shape, F32)
    for j, val in ((R_E0, e0.astype(F32)), (R_E1, e1.astype(F32)), (R_RANK0, rank0),
                   (R_RANK1, rank1), (R_W0, w0), (R_W1, w1)):
        route = jnp.where(lane == j, val, route)
    route_ref[...] = route
    base = base_ref[...] + jnp.sum(oh, axis=0, keepdims=True)
    base_ref[...] = base
    cum_ref[...] = base


def _router(x, mod3, row_of, g, wr_pad, tb):
    m, d = x.shape
    nb = m // tb
    return pl.pallas_call(
        _router_kernel,
        grid=(nb,),
        in_specs=[pl.BlockSpec((tb, d), lambda i: (i, 0)),
                  pl.BlockSpec((None, 1, mod3.shape[2]), lambda i: (row_of(i), 0, 0)),
                  pl.BlockSpec((1, d), lambda i: (0, 0)),
                  pl.BlockSpec((d, LANES), lambda i: (0, 0))],
        out_specs=[pl.BlockSpec((tb, d), lambda i: (i, 0)),
                   pl.BlockSpec((tb, LANES), lambda i: (i, 0)),
                   pl.BlockSpec((None, 1, LANES), lambda i: (i, 0, 0))],
        out_shape=[jax.ShapeDtypeStruct((m, d), BF16),
                   jax.ShapeDtypeStruct((m, LANES), F32),
                   jax.ShapeDtypeStruct((nb, 1, LANES), F32)],
        scratch_shapes=[pltpu.VMEM((1, LANES), F32)],
        compiler_params=_cparams(("arbitrary",)),
        name="moe_router",
    )(x, mod3, g, wr_pad)


def _moe_plan(route, cum, nb, tb, tw):
    i32 = jnp.int32
    wpt = tb // tw
    span = wpt + 1
    n_tiles = 2 * nb + N_EXPERTS
    n_items = n_tiles * wpt + N_EXPERTS * nb
    cum_in = cum[:, 0, :N_EXPERTS].astype(i32)
    cum_ex = jnp.concatenate([jnp.zeros((1, N_EXPERTS), i32), cum_in[:-1]], axis=0)
    counts = cum_in[-1]
    ntile = (counts + tb - 1) // tb
    tile_end = jnp.cumsum(ntile)
    start = (tile_end - ntile) * tb
    n_used = tile_end[-1]
    t_idx = jnp.arange(n_tiles, dtype=i32)
    tile_valid = (t_idx < n_used).astype(i32)
    tile_src = jnp.minimum(t_idx, n_used - 1)
    tile_expert = jnp.minimum(jnp.sum((tile_src[:, None] >= tile_end[None, :]).astype(i32), axis=1),
                              N_EXPERTS - 1)
    e0 = route[:, R_E0].astype(i32)
    e1 = route[:, R_E1].astype(i32)
    pos0 = start[e0] + route[:, R_RANK0].astype(i32)
    pos1 = start[e1] + route[:, R_RANK1].astype(i32)
    lo = start[None, :] + cum_ex
    hi = start[None, :] + cum_in
    w_lo = lo // tw
    npair = jnp.where(hi > lo, (hi - 1) // tw - w_lo + 1, 0)
    j = jnp.arange(span, dtype=i32)
    c_valid = j[None, None, :] < npair[:, :, None]
    c_win = w_lo[:, :, None] + j[None, None, :]
    c_blk = jnp.broadcast_to(jnp.arange(nb, dtype=i32)[:, None, None], c_valid.shape)
    jp = jnp.arange(wpt - 1, dtype=i32)
    p_win = ((start + counts + tw - 1) // tw)[:, None] + jp[None, :]
    p_valid = p_win < (tile_end * wpt)[:, None]

    def compact(valid, cols, n):
        order = jnp.argsort(jnp.logical_not(valid), stable=True)[:n]
        n_valid = jnp.sum(valid.astype(i32))
        k = jnp.minimum(jnp.arange(n, dtype=i32), n_valid - 1)
        return [c[order][k] for c in cols], jnp.arange(n, dtype=i32) < n_valid

    def flags(valid, key, extra):
        prev = jnp.concatenate([key[:1] - 1, key[:-1]])
        nxt = jnp.concatenate([key[1:], key[-1:] - 1])
        last_valid = jnp.concatenate([jnp.logical_not(valid[1:]), jnp.ones((1,), bool)])
        return (valid.astype(i32) * ITEM_VALID
                + (valid & (key != prev)).astype(i32) * ITEM_FIRST
                + (valid & ((key != nxt) | last_valid)).astype(i32) * ITEM_LAST
                + (valid & extra).astype(i32) * ITEM_ZERO)

    em = lambda a: jnp.transpose(a, (1, 0, 2)).reshape(-1)
    bm = lambda a: a.reshape(-1)
    n_pad = N_EXPERTS * (wpt - 1)
    g_valid = jnp.concatenate([em(c_valid), p_valid.reshape(-1)])
    g_win = jnp.concatenate([em(c_win), p_win.reshape(-1)])
    g_blk = jnp.concatenate([em(c_blk), jnp.zeros((n_pad,), i32)])
    g_zero = jnp.concatenate([jnp.zeros(em(c_valid).shape, bool), jnp.ones((n_pad,), bool)])
    (g_win, g_blk, g_zero), g_ok = compact(g_valid, [g_win, g_blk, g_zero], n_items + n_pad)
    (c_win2, c_blk2), c_ok = compact(bm(c_valid), [bm(c_win), bm(c_blk)], n_items)
    return dict(pos0=pos0, pos1=pos1, start=start, tile_src=tile_src, tile_expert=tile_expert,
                tile_valid=tile_valid, n_tiles=n_tiles,
                gather=(g_win, g_blk, flags(g_ok, g_win, g_zero)),
                combine=(c_blk2, c_win2, flags(c_ok, c_blk2, jnp.zeros_like(c_ok))))


def _gather_kernel(it_win, it_blk, it_flag, p0_ref, p1_ref, h_ref, o_ref, *, tb, tw):
    k = pl.program_id(0)
    flag = it_flag[k]

    @pl.when((flag & ITEM_ZERO) != 0)
    def _():
        o_ref[...] = jnp.zeros(o_ref.shape, BF16)

    @pl.when(((flag & ITEM_VALID) != 0) & ((flag & ITEM_ZERO) == 0))
    def _():
        rows = lax.broadcasted_iota(jnp.int32, (tw, tb), 0) + it_win[k] * tw
        sel = jnp.where(p0_ref[...] == rows, 1.0, 0.0) + jnp.where(p1_ref[...] == rows, 1.0, 0.0)
        c = jnp.dot(sel.astype(BF16), h_ref[...], preferred_element_type=F32)

        @pl.when((flag & ITEM_FIRST) != 0)
        def _():
            o_ref[...] = c.astype(BF16)

        @pl.when((flag & ITEM_FIRST) == 0)
        def _():
            o_ref[...] = (o_ref[...].astype(F32) + c).astype(BF16)


def _moe_gather(h, plan, tb, tw):
    m, d = h.shape
    nb = m // tb
    it_win, it_blk, it_flag = plan["gather"]
    p0 = plan["pos0"].reshape(nb, 1, tb)
    p1 = plan["pos1"].reshape(nb, 1, tb)
    grid_spec = pltpu.PrefetchScalarGridSpec(
        num_scalar_prefetch=3,
        grid=(it_win.shape[0],),
        in_specs=[pl.BlockSpec((None, 1, tb), lambda k, w, b, f: (b[k], 0, 0)),
                  pl.BlockSpec((None, 1, tb), lambda k, w, b, f: (b[k], 0, 0)),
                  pl.BlockSpec((tb, d), lambda k, w, b, f: (b[k], 0))],
        out_specs=pl.BlockSpec((tw, d), lambda k, w, b, f: (w[k], 0)),
    )
    return pl.pallas_call(
        functools.partial(_gather_kernel, tb=tb, tw=tw),
        grid_spec=grid_spec,
        out_shape=jax.ShapeDtypeStruct((plan["n_tiles"] * tb, d), BF16),
        compiler_params=_cparams(("arbitrary",)),
        name="moe_gather",
    )(it_win, it_blk, it_flag, p0, p1, h)


def _expert_kernel(ts_ref, te_ref, tv_ref, x_ref, wg_ref, wu_ref, wd_ref, o_ref, acc_ref, *, nf):
    t = pl.program_id(0)
    f = pl.program_id(1)

    @pl.when(tv_ref[t] != 0)
    def _():
        xb = x_ref[...]
        a = jnp.dot(xb, wg_ref[...], preferred_element_type=F32)
        b = jnp.dot(xb, wu_ref[...], preferred_element_type=F32)
        y = jnp.dot((_silu(a) * b).astype(BF16), wd_ref[...], preferred_element_type=F32)

        @pl.when(f == 0)
        def _():
            acc_ref[...] = y

        @pl.when(f != 0)
        def _():
            acc_ref[...] += y

        @pl.when(f == nf - 1)
        def _():
            o_ref[...] = acc_ref[...].astype(BF16)


def _moe_experts(xs, plan, wg, wu, wd, tb, tf):
    d = xs.shape[1]
    nf = wg.shape[2] // tf
    fsel = lambda f, tv, t: jnp.where(tv[t] != 0, f, nf - 1)
    grid_spec = pltpu.PrefetchScalarGridSpec(
        num_scalar_prefetch=3,
        grid=(plan["n_tiles"], nf),
        in_specs=[pl.BlockSpec((tb, d), lambda t, f, ts, te, tv: (ts[t], 0)),
                  pl.BlockSpec((None, d, tf), lambda t, f, ts, te, tv: (te[t], 0, fsel(f, tv, t))),
                  pl.BlockSpec((None, d, tf), lambda t, f, ts, te, tv: (te[t], 0, fsel(f, tv, t))),
                  pl.BlockSpec((None, tf, d), lambda t, f, ts, te, tv: (te[t], fsel(f, tv, t), 0))],
        out_specs=pl.BlockSpec((tb, d), lambda t, f, ts, te, tv: (ts[t], 0)),
        scratch_shapes=[pltpu.VMEM((tb, d), F32)],
    )
    return pl.pallas_call(
        functools.partial(_expert_kernel, nf=nf),
        grid_spec=grid_spec,
        out_shape=jax.ShapeDtypeStruct(xs.shape, BF16),
        compiler_params=_cparams(("arbitrary", "arbitrary")),
        name="moe_experts",
    )(plan["tile_src"], plan["tile_expert"], plan["tile_valid"], xs, wg, wu, wd)


def _combine_kernel(it_blk, it_win, it_flag, x_ref, mod_ref, route_ref, start_ref, y_ref,
                    o_ref, acc_ref, *, tb, tw):
    k = pl.program_id(0)
    flag = it_flag[k]

    @pl.when((flag & ITEM_VALID) != 0)
    def _():
        route = route_ref[...]
        lanef = lax.broadcasted_iota(jnp.int32, route.shape, 1).astype(F32)
        start = start_ref[...]
        base = (it_win[k] * tw).astype(F32)

        def slot(je, jr, jw):
            e = route[:, je:je + 1]
            s = jnp.sum(jnp.where(lanef == e, start, 0.0), axis=1, keepdims=True)
            return s + route[:, jr:jr + 1] - base, route[:, jw:jw + 1]

        pos0, w0 = slot(R_E0, R_RANK0, R_W0)
        pos1, w1 = slot(R_E1, R_RANK1, R_W1)
        col = lax.broadcasted_iota(jnp.int32, (tb, tw), 1).astype(F32)
        sel = jnp.where(col == pos0, w0, 0.0) + jnp.where(col == pos1, w1, 0.0)
        c = jnp.dot(sel.astype(BF16), y_ref[...], preferred_element_type=F32)

        @pl.when((flag & ITEM_FIRST) != 0)
        def _():
            acc_ref[...] = c

        @pl.when((flag & ITEM_FIRST) == 0)
        def _():
            acc_ref[...] += c

        @pl.when((flag & ITEM_LAST) != 0)
        def _():
            o_ref[...] = x_ref[...] + mod_ref[:, 5 * D_MODEL:6 * D_MODEL] * acc_ref[...]


def _moe_combine(x, mod3, row_of, route, ys, plan, tb, tw):
    m, d = x.shape
    it_blk, it_win, it_flag = plan["combine"]
    start = jnp.zeros((1, LANES), F32).at[0, :N_EXPERTS].set(plan["start"].astype(F32))
    grid_spec = pltpu.PrefetchScalarGridSpec(
        num_scalar_prefetch=3,
        grid=(it_blk.shape[0],),
        in_specs=[pl.BlockSpec((tb, d), lambda k, b, w, f: (b[k], 0)),
                  pl.BlockSpec((None, 1, mod3.shape[2]), lambda k, b, w, f: (row_of(b[k]), 0, 0)),
                  pl.BlockSpec((tb, LANES), lambda k, b, w, f: (b[k], 0)),
                  pl.BlockSpec((1, LANES), lambda k, b, w, f: (0, 0)),
                  pl.BlockSpec((tw, d), lambda k, b, w, f: (w[k], 0))],
        out_specs=pl.BlockSpec((tb, d), lambda k, b, w, f: (b[k], 0)),
        scratch_shapes=[pltpu.VMEM((tb, d), F32)],
    )
    return pl.pallas_call(
        functools.partial(_combine_kernel, tb=tb, tw=tw),
        grid_spec=grid_spec,
        out_shape=jax.ShapeDtypeStruct((m, d), F32),
        compiler_params=_cparams(("arbitrary",)),
        name="moe_combine",
    )(it_blk, it_win, it_flag, x, mod3, route, start, ys)


def _moe_ffn(x, mod3, row_of, g, wr_pad, wg, wu, wd, tb, tf):
    nb = x.shape[0] // tb
    tw = min(MOE_WINDOW, tb)
    h, route, cum = _router(x, mod3, row_of, g, wr_pad, tb)
    plan = _moe_plan(route, cum, nb, tb, tw)
    xs = _moe_gather(h, plan, tb, tw)
    ys = _moe_experts(xs, plan, wg, wu, wd, tb, tf)
    return _moe_combine(x, mod3, row_of, route, ys, plan, tb, tw)


def _rope_tables(seq):
    t = jnp.arange(seq, dtype=jnp.int32)
    row = (t // GRID_W).astype(F32)
    col = (t % GRID_W).astype(F32)
    inv = ROPE_THETA ** (-jnp.arange(0, ROT_AXIS, 2, dtype=F32) / ROT_AXIS)
    ar = row[:, None] * inv
    ac = col[:, None] * inv
    ang = jnp.concatenate([ar, ar, ac, ac], axis=-1)
    ang = jnp.concatenate([ang, ang], axis=-1)
    cos, sin = jnp.cos(ang), jnp.sin(ang)
    upper = (jnp.arange(HEAD_W) & ROT_HALF) != 0
    sa = jnp.where(upper, 0.0, -sin)
    sb = jnp.where(upper, sin, 0.0)
    return cos, sa, sb


def _pick_tile(n, pref):
    t = min(n, pref)
    while n % t:
        t //= 2
    return t


def kernel(x, c, ctx, c_ctx, w_mod, b_mod, g_mix, w_in, q_norm_g, k_norm_g, lambda_q1, lambda_k1, lambda_q2, lambda_k2, subln_g, w_attn_o, dw_weight, dw_bias, conv_ln_g, conv_ln_b, w_conv_o, w_out, g_ffn, w_ff_gate, w_ff_up, w_ff_down, w_router, w_exp_gate, w_exp_up, w_exp_down):
    n_batch, seq, d = x.shape
    ctx_len = ctx.shape[1]
    depth = w_mod.shape[0]
    assert d == D_MODEL and seq % GRID_W == 0

    tm_l = _pick_tile(seq, 512)
    tm_c = _pick_tile(ctx_len, 512)
    ctx_row = n_batch
    mod_rows = -(-(n_batch + 1) // 8) * 8
    cvec = jnp.zeros((mod_rows, d), F32).at[:n_batch].set(c).at[ctx_row].set(c_ctx)

    cos, sa, sb = _rope_tables(seq)
    gid = jnp.arange(HEAD_W) // HEAD_DIM
    gsum = ((gid[:, None] == gid[None, :]).astype(F32) / HEAD_DIM).astype(BF16)

    def rope_with_gain(g_row):
        return (cos * g_row, sa * jnp.roll(g_row, -ROT_HALF, axis=1),
                sb * jnp.roll(g_row, ROT_HALF, axis=1))

    xl = x.reshape(n_batch * seq, d)
    xc = ctx.reshape(n_batch * ctx_len, d)
    lat_row = lambda i: (i * tm_l) // seq
    ctx_row_of = lambda i: ctx_row

    for l in range(depth):
        last = l == depth - 1
        lam_init = 0.8 - 0.6 * math.exp(-0.3 * l)
        mod3 = _modulation(cvec, w_mod[l], b_mod[l]).reshape(mod_rows, 1, 6 * d)
        w_bf = w_in[l].astype(BF16)
        gmix = g_mix[l].reshape(1, d)
        qg = jnp.tile(q_norm_g[l], 2).reshape(1, HEAD_W) * Q_SCALE
        kg = jnp.tile(k_norm_g[l], 2).reshape(1, HEAD_W)
        tabs = rope_with_gain(qg) + rope_with_gain(kg)
        lamv = jnp.stack([lambda_q1[l], lambda_k1[l], lambda_q2[l], lambda_k2[l]])
        sg = subln_g[l].reshape(HEAD_W, 1)
        dw = jnp.zeros((32, d), F32).at[:CONV_K].set(dw_weight[l])
        merge_w = (dw, dw_bias[l].reshape(1, d), conv_ln_g[l].reshape(1, d),
                   conv_ln_b[l].reshape(1, d), w_conv_o[l].astype(BF16),
                   w_attn_o[l].astype(BF16), w_out[l].astype(BF16))

        q, k, v, u, gates = _in_proj(xl, mod3, lat_row, gmix, w_bf, qg, kg, gsum, tabs,
                                     seq, 0, N_GROUPS, tm_l)
        if last:
            kc, vc = _in_proj(xc, mod3, ctx_row_of, gmix, w_bf, qg, kg, gsum, None,
                              ctx_len, 1, 2, tm_c)
        else:
            qc, kc, vc, uc, gatesc = _in_proj(xc, mod3, ctx_row_of, gmix, w_bf, qg, kg, gsum,
                                              None, ctx_len, 0, N_GROUPS, tm_c)
        tk = ATTN_TK
        ao = _diff_attention(q, [(k, v), (kc, vc)], lamv, sg, n_batch, seq,
                             _pick_tile(seq, ATTN_TQ), tk, lam_init)
        xl = _merge(xl, ao, u, gates, mod3, lat_row, *merge_w, seq, tm_l)
        if not last:
            aoc = _diff_attention(qc, [(kc, vc)], lamv, sg, n_batch, ctx_len,
                                  _pick_tile(ctx_len, ATTN_TQ), tk, lam_init)
            xc = _merge(xc, aoc, uc, gatesc, mod3, ctx_row_of, *merge_w, ctx_len, tm_c)

        gf = g_ffn[l].reshape(1, d)
        i = l // 2
        if l % 2 == 0:
            wts = (w_ff_gate[i].astype(BF16), w_ff_up[i].astype(BF16), w_ff_down[i].astype(BF16))
            tf = _pick_ff_tile(wts[0].shape[1])
            xl = _dense_ffn(xl, mod3, lat_row, gf, *wts, tm_l, tf)
            if not last:
                xc = _dense_ffn(xc, mod3, ctx_row_of, gf, *wts, tm_c, tf)
        else:
            wr_pad = jnp.zeros((d, LANES), F32).at[:, :N_EXPERTS].set(w_router[i])
            wts = (w_exp_gate[i].astype(BF16), w_exp_up[i].astype(BF16),
                   w_exp_down[i].astype(BF16))
            tf = _pick_ff_tile(wts[0].shape[2])
            xl = _moe_ffn(xl, mod3, lat_row, gf, wr_pad, *wts, tm_l, tf)
            if not last:
                xc = _moe_ffn(xc, mod3, ctx_row_of, gf, wr_pad, *wts, tm_c, tf)
    return xl.reshape(n_batch, seq, d)


def _pick_ff_tile(dff):
    best = LANES
    for t in range(LANES, 1792 + 1, LANES):
        if dff % t == 0:
            best = t
    return best
```

```python
import functools
import math

import jax
import jax.numpy as jnp
from jax import lax
from jax.experimental import pallas as pl
from jax.experimental.pallas import tpu as pltpu

F32 = jnp.float32
BF16 = jnp.bfloat16

LANES = 128
D_MODEL = 1024
N_HEADS = 8
HEAD_DIM = 64
HEAD_W = 2 * HEAD_DIM
GRID_W = 64
ROT_AXIS = HEAD_DIM // 2
ROT_HALF = ROT_AXIS // 2
ROPE_THETA = 10000.0
CONV_K = 31
CONV_HALO = 16
N_EXPERTS = 8
EPS = 1e-6
N_GROUPS = 7
ATTN_TQ = 512
ATTN_TK = 512
MOE_WINDOW = 256
KNORM_ROWS = 512
BOUND_SLACK = 1.01
MIN_DENOM = 2.0 ** -60
VMEM_LIMIT = 52 * 1024 * 1024
Q_SCALE = (HEAD_DIM ** -0.5) * math.log2(math.e)
NEG_BIG = -1e30


def _cparams(sem):
    return pltpu.CompilerParams(dimension_semantics=sem, vmem_limit_bytes=VMEM_LIMIT)


def _sigmoid(v):
    return 0.5 * jnp.tanh(0.5 * v) + 0.5


def _silu(v):
    return v * _sigmoid(v)


def _norm_modulate(xf, g, shift, scale):
    ms = jnp.mean(xf * xf, axis=-1, keepdims=True)
    return (xf * lax.rsqrt(ms + EPS) * g) * (1.0 + scale) + shift


def _mod_kernel(c_ref, w_ref, b_ref, o_ref):
    a = _silu(c_ref[...])
    o_ref[...] = jnp.dot(a, w_ref[...], preferred_element_type=F32,
                         precision=lax.Precision.HIGHEST) + b_ref[...]


def _modulation(cvec, w, b):
    rows, d = cvec.shape
    n = w.shape[1]
    tn = n // 4
    return pl.pallas_call(
        _mod_kernel,
        grid=(n // tn,),
        in_specs=[pl.BlockSpec((rows, d), lambda j: (0, 0)),
                  pl.BlockSpec((d, tn), lambda j: (0, j)),
                  pl.BlockSpec((1, tn), lambda j: (0, j))],
        out_specs=pl.BlockSpec((rows, tn), lambda j: (0, j)),
        out_shape=jax.ShapeDtypeStruct((rows, n), F32),
        compiler_params=_cparams(("arbitrary",)),
        name="modulation",
    )(cvec, w, b.reshape(1, n))


G_Q, G_K, G_V, G_GLU_A, G_GLU_G, G_GATE_A, G_GATE_C = range(N_GROUPS)
GROUPS_QK = (G_Q, G_K)
GROUPS_REST = (G_V, G_GLU_A, G_GLU_G, G_GATE_A, G_GATE_C)
GROUPS_KV = (G_K, G_V)


def _proj_kernel(*refs, groups, from_x, rope):
    if from_x:
        x_ref, mod_ref, gmix_ref = refs[:3]
        pos = 3
    else:
        hin_ref = refs[0]
        pos = 1
    w_refs = refs[pos:pos + len(groups)]
    pos += len(groups)
    qg_ref, kg_ref, gsum_ref = refs[pos:pos + 3]
    pos += 3
    tabs = {}
    if rope:
        for g in (G_Q, G_K):
            if g in groups:
                tabs[g] = refs[pos:pos + 3]
                pos += 3
    outs = list(refs[pos:])
    if from_x:
        h = _norm_modulate(x_ref[...], gmix_ref[...], mod_ref[:, 0:D_MODEL],
                           mod_ref[:, D_MODEL:2 * D_MODEL]).astype(BF16)
        h_out = outs.pop(0)
        h_out[...] = h
    else:
        h = hin_ref[...]
    out_of = {}
    for g in groups:
        if g in (G_Q, G_K, G_V):
            out_of[g] = outs.pop(0)
    if G_GLU_A in groups:
        u_ref = outs.pop(0)
    if G_GATE_A in groups:
        gates_ref = outs.pop(0)

    def qk_norm(acc, out_ref, g_row_ref, tables):
        gsum = gsum_ref[...]
        for hb in range(N_HEADS):
            sl = slice(hb * HEAD_W, (hb + 1) * HEAD_W)
            blk = acc[:, sl]
            ms = jnp.dot((blk * blk).astype(BF16), gsum, preferred_element_type=F32)
            y = blk * lax.rsqrt(ms + EPS)
            if tables is not None:
                c_ref, sa_ref, sb_ref = tables
                y = (y * c_ref[...] + pltpu.roll(y, HEAD_W - ROT_HALF, 1) * sa_ref[...]
                     + pltpu.roll(y, ROT_HALF, 1) * sb_ref[...])
            else:
                y = y * g_row_ref[...]
            out_ref[:, sl] = y.astype(BF16)

    glu_a = None
    for g, w_ref in zip(groups, w_refs):
        acc = jnp.dot(h, w_ref[...], preferred_element_type=F32)
        if g == G_Q:
            qk_norm(acc, out_of[g], qg_ref, tabs.get(g))
        elif g == G_K:
            qk_norm(acc, out_of[g], kg_ref, tabs.get(g))
        elif g == G_V:
            out_of[g][...] = acc.T.astype(BF16)
        elif g == G_GLU_A:
            glu_a = acc
        elif g == G_GLU_G:
            u_ref[...] = glu_a * _sigmoid(acc)
        elif g == G_GATE_A:
            gates_ref[:, 0:D_MODEL] = _sigmoid(acc).astype(BF16)
        elif g == G_GATE_C:
            gates_ref[:, D_MODEL:2 * D_MODEL] = _sigmoid(acc).astype(BF16)


def _proj(src, mod3, row_of, gmix, w_bf, qg, kg, gsum, rope_tabs, seq, groups, from_x, tm):
    m, d = src.shape
    tps = seq // tm
    rope = rope_tabs is not None
    const = lambda i: (0, 0)
    row = lambda i: (i, 0)
    if from_x:
        in_specs = [pl.BlockSpec((tm, d), row),
                    pl.BlockSpec((None, 1, mod3.shape[2]), lambda i: (row_of(i), 0, 0)),
                    pl.BlockSpec((1, d), const)]
        args = [src, mod3, gmix]
    else:
        in_specs = [pl.BlockSpec((tm, d), row)]
        args = [src]
    for g in groups:
        in_specs.append(pl.BlockSpec((d, D_MODEL), lambda i, g=g: (0, g)))
        args.append(w_bf)
    in_specs += [pl.BlockSpec((1, HEAD_W), const), pl.BlockSpec((1, HEAD_W), const),
                 pl.BlockSpec((HEAD_W, HEAD_W), const)]
    args += [qg, kg, gsum]
    if rope:
        for g, t3 in ((G_Q, rope_tabs[0:3]), (G_K, rope_tabs[3:6])):
            if g in groups:
                for t in t3:
                    in_specs.append(pl.BlockSpec((tm, HEAD_W), lambda i: (i % tps, 0)))
                    args.append(t)
    row_spec = pl.BlockSpec((tm, D_MODEL), row)
    rows_bf = jax.ShapeDtypeStruct((m, D_MODEL), BF16)
    out_specs, out_shape = [], []
    if from_x:
        out_specs.append(row_spec)
        out_shape.append(rows_bf)
    for g in groups:
        if g in (G_Q, G_K):
            out_specs.append(row_spec)
            out_shape.append(rows_bf)
        elif g == G_V:
            out_specs.append(pl.BlockSpec((D_MODEL, tm), lambda i: (i // tps, i % tps)))
            out_shape.append(jax.ShapeDtypeStruct(((m // seq) * D_MODEL, seq), BF16))
    if G_GLU_A in groups:
        out_specs.append(row_spec)
        out_shape.append(jax.ShapeDtypeStruct((m, D_MODEL), F32))
    if G_GATE_A in groups:
        out_specs.append(pl.BlockSpec((tm, 2 * D_MODEL), row))
        out_shape.append(jax.ShapeDtypeStruct((m, 2 * D_MODEL), BF16))
    return pl.pallas_call(
        functools.partial(_proj_kernel, groups=groups, from_x=from_x, rope=rope),
        grid=(m // tm,),
        in_specs=in_specs,
        out_specs=out_specs,
        out_shape=out_shape,
        compiler_params=_cparams(("parallel",)),
        name="in_proj",
    )(*args)


def _in_proj_full(x, mod3, row_of, gmix, w_bf, qg, kg, gsum, rope_tabs, seq, tm):
    h, q, k = _proj(x, mod3, row_of, gmix, w_bf, qg, kg, gsum, rope_tabs, seq, GROUPS_QK, True, tm)
    vt, u, gates = _proj(h, mod3, row_of, gmix, w_bf, qg, kg, gsum, None, seq, GROUPS_REST,
                         False, tm)
    return q, k, vt, u, gates


def _attn_kernel(*refs, src_lens, tq, tk, lam_init):
    n_src = len(src_lens)
    q_ref = refs[0]
    kv_refs = refs[1:1 + 2 * n_src]
    lamv_ref, sg_ref, ksum_ref, o_ref, acc_ref, kmax_ref = refs[1 + 2 * n_src:7 + 2 * n_src]
    s_refs = refs[7 + 2 * n_src:9 + 2 * n_src]
    n_cb = 2 * tq // LANES

    @pl.when(pl.program_id(2) == 0)
    def _():
        best = jnp.zeros((1, HEAD_W), F32)
        for si in range(n_src):
            for off in range(0, src_lens[si], KNORM_ROWS):
                rows = min(KNORM_ROWS, src_lens[si] - off)
                kf = kv_refs[2 * si][off:off + rows, :].astype(F32)
                n2 = jnp.dot((kf * kf).astype(BF16), ksum_ref[...], preferred_element_type=F32)
                best = jnp.maximum(best, jnp.max(n2, axis=0, keepdims=True))
        kmax_ref[...] = jnp.sqrt(best) * BOUND_SLACK

    q = q_ref[...]
    lane = lax.broadcasted_iota(jnp.int32, q.shape, 1)
    zero = jnp.zeros_like(q)
    q2 = jnp.concatenate([jnp.where(lane < HEAD_DIM, q, zero),
                          jnp.where(lane >= HEAD_DIM, q, zero)], axis=0)
    chunks = [(si, off, min(tk, src_lens[si])) for si in range(n_src)
              for off in range(0, src_lens[si], min(tk, src_lens[si]))]

    def scores(si, off, tkk):
        kc = kv_refs[2 * si][pl.ds(off, tkk), :]
        return lax.dot_general(kc, q2, (((1,), (1,)), ((), ())), preferred_element_type=F32)

    def finalize(acc, l):
        lv = lamv_ref[...]
        lam = (jnp.exp(jnp.sum(lv[0:1] * lv[1:2], axis=1, keepdims=True))
               - jnp.exp(jnp.sum(lv[2:3] * lv[3:4], axis=1, keepdims=True)) + lam_init)
        inv = 1.0 / l
        o_t = acc[:, 0:tq] * inv[:, 0:tq] - lam * (acc[:, tq:2 * tq] * inv[:, tq:2 * tq])
        ms = jnp.mean(o_t * o_t, axis=0, keepdims=True)
        y_t = o_t * lax.rsqrt(ms + EPS) * (sg_ref[...] * (1.0 - lam_init))
        o_ref[...] = y_t.T.astype(BF16)

    q2f = q2.astype(F32)
    qn2 = lax.dot_general(jnp.ones((8, HEAD_W), BF16), (q2f * q2f).astype(BF16),
                          (((1,), (1,)), ((), ())), preferred_element_type=F32)[0:1, :]
    kmax = kmax_ref[...]
    kmax_row = jnp.concatenate([jnp.broadcast_to(kmax[:, 0:1], (1, tq)),
                                jnp.broadcast_to(kmax[:, HEAD_DIM:HEAD_DIM + 1], (1, tq))], axis=1)
    bound = jnp.sqrt(qn2) * BOUND_SLACK * kmax_row
    s_refs[0][0:chunks[0][2], :] = scores(*chunks[0])
    l = jnp.zeros((1, 2 * tq), F32)
    for ci, (si, off, tkk) in enumerate(chunks):
        if ci + 1 < len(chunks):
            s_refs[(ci + 1) % 2][0:chunks[ci + 1][2], :] = scores(*chunks[ci + 1])
        p = jnp.exp2(s_refs[ci % 2][0:tkk, :] - bound)
        l = l + jnp.sum(p, axis=0, keepdims=True)
        pv = jnp.dot(kv_refs[2 * si + 1][:, pl.ds(off, tkk)], p.astype(BF16),
                     preferred_element_type=F32)
        if ci == 0:
            acc_ref[...] = pv
        else:
            acc_ref[...] += pv
    healthy = jnp.min(l) >= MIN_DENOM

    @pl.when(healthy)
    def _():
        finalize(acc_ref[...], l)

    @pl.when(jnp.logical_not(healthy))
    def _():
        acc_ref[...] = jnp.zeros(acc_ref.shape, F32)
        m = [jnp.full((1, LANES), NEG_BIG, F32) for _ in range(n_cb)]
        lx = [jnp.zeros((1, LANES), F32) for _ in range(n_cb)]
        for si, off, tkk in chunks:
            s = scores(si, off, tkk)
            alphas, ps = [], []
            for cb in range(n_cb):
                s_cb = s[:, cb * LANES:(cb + 1) * LANES]
                m_new = jnp.maximum(m[cb], jnp.max(s_cb, axis=0, keepdims=True))
                alpha = jnp.exp2(m[cb] - m_new)
                p_cb = jnp.exp2(s_cb - m_new)
                lx[cb] = alpha * lx[cb] + jnp.sum(p_cb, axis=0, keepdims=True)
                alphas.append(alpha)
                ps.append(p_cb.astype(BF16))
                m[cb] = m_new
            acc_ref[...] = (jnp.concatenate(alphas, axis=1) * acc_ref[...]
                            + jnp.dot(kv_refs[2 * si + 1][:, pl.ds(off, tkk)],
                                      jnp.concatenate(ps, axis=1), preferred_element_type=F32))
        finalize(acc_ref[...], jnp.concatenate(lx, axis=1))


def _diff_attention(q, kvs, lamv, sg_col, n_batch, sq, tq, tk, lam_init):
    m = q.shape[0]
    nq = sq // tq
    in_specs = [pl.BlockSpec((tq, HEAD_W), lambda b, h, i: (b * nq + i, h))]
    args = [q]
    src_lens = []
    for (k, vt) in kvs:
        sl = k.shape[0] // n_batch
        src_lens.append(sl)
        in_specs.append(pl.BlockSpec((sl, HEAD_W), lambda b, h, i: (b, h)))
        in_specs.append(pl.BlockSpec((HEAD_W, sl), lambda b, h, i: (b * N_HEADS + h, 0)))
        args += [k, vt]
    in_specs += [pl.BlockSpec(lamv.shape, lambda b, h, i: (0, 0)),
                 pl.BlockSpec((HEAD_W, 1), lambda b, h, i: (0, 0)),
                 pl.BlockSpec((HEAD_W, HEAD_W), lambda b, h, i: (0, 0))]
    gid = jnp.arange(HEAD_W) // HEAD_DIM
    args += [lamv, sg_col, (gid[:, None] == gid[None, :]).astype(BF16)]
    return pl.pallas_call(
        functools.partial(_attn_kernel, src_lens=tuple(src_lens), tq=tq, tk=tk,
                          lam_init=lam_init),
        grid=(n_batch, N_HEADS, nq),
        in_specs=in_specs,
        out_specs=pl.BlockSpec((tq, HEAD_W), lambda b, h, i: (b * nq + i, h)),
        out_shape=jax.ShapeDtypeStruct((m, N_HEADS * HEAD_W), BF16),
        scratch_shapes=[pltpu.VMEM((HEAD_W, 2 * tq), F32), pltpu.VMEM((1, HEAD_W), F32),
                        pltpu.VMEM((tk, 2 * tq), F32), pltpu.VMEM((tk, 2 * tq), F32)],
        compiler_params=_cparams(("parallel", "parallel", "arbitrary")),
        name="diff_attention",
    )(*args)


def _merge_kernel(x_ref, ao_ref, u_ref, up_ref, un_ref, gates_ref, mod_ref, dw_ref, dwb_ref,
                  lng_ref, lnb_ref, wco_ref, wao_ref, wout_ref, o_ref, ubuf, cbuf, *, tm, tps, rc):
    i = pl.program_id(0)
    first = (i % tps) == 0
    last = (i % tps) == tps - 1
    ubuf[0:CONV_HALO, :] = jnp.where(first, 0.0, up_ref[...])
    ubuf[CONV_HALO:CONV_HALO + tm, :] = u_ref[...]
    ubuf[CONV_HALO + tm:2 * CONV_HALO + tm, :] = jnp.where(last, 0.0, un_ref[...])
    shift = CONV_HALO - CONV_K // 2
    win = rc + 2 * CONV_HALO

    def chunk(r, carry):
        r0 = pl.multiple_of(r * rc, rc)
        for cb in range(D_MODEL // LANES):
            sl = slice(cb * LANES, (cb + 1) * LANES)
            w = ubuf[pl.ds(r0, win), sl]
            acc = jnp.zeros((rc, LANES), F32)
            for rot in range(8):
                wr = w if rot == 0 else pltpu.roll(w, win - rot, 0)
                for k in range(CONV_K):
                    if (k + shift) % 8 == rot:
                        a0 = (k + shift) - rot
                        acc = acc + wr[a0:a0 + rc, :] * dw_ref[k:k + 1, sl]
            cbuf[pl.ds(r0, rc), sl] = acc + dwb_ref[:, sl]
        return carry

    lax.fori_loop(0, tm // rc, chunk, 0)

    cv = cbuf[...]
    mu = jnp.mean(cv, axis=-1, keepdims=True)
    xc = cv - mu
    var = jnp.mean(xc * xc, axis=-1, keepdims=True)
    y = xc * lax.rsqrt(var + EPS) * lng_ref[...] + lnb_ref[...]
    y_conv = jnp.dot(_silu(y).astype(BF16), wco_ref[...], preferred_element_type=F32)
    y_attn = jnp.dot(ao_ref[...], wao_ref[...], preferred_element_type=F32)
    mix = (gates_ref[:, 0:D_MODEL].astype(F32) * y_attn
           + gates_ref[:, D_MODEL:2 * D_MODEL].astype(F32) * y_conv)
    out = jnp.dot(mix.astype(BF16), wout_ref[...], preferred_element_type=F32)
    o_ref[...] = x_ref[...] + mod_ref[:, 2 * D_MODEL:3 * D_MODEL] * out


def _merge(x, ao, u, gates, mod3, row_of, dw, dwb, lng, lnb, wco, wao, wout, seq, tm):
    m, d = x.shape
    tps = seq // tm
    hb = tm // CONV_HALO
    n_halo = m // CONV_HALO
    row = lambda i: (i, 0)
    const = lambda i: (0, 0)
    in_specs = [
        pl.BlockSpec((tm, d), row),
        pl.BlockSpec((tm, d), row),
        pl.BlockSpec((tm, d), row),
        pl.BlockSpec((CONV_HALO, d), lambda i: (jnp.maximum(i * hb - 1, 0), 0)),
        pl.BlockSpec((CONV_HALO, d), lambda i: (jnp.minimum((i + 1) * hb, n_halo - 1), 0)),
        pl.BlockSpec((tm, 2 * d), row),
        pl.BlockSpec((None, 1, mod3.shape[2]), lambda i: (row_of(i), 0, 0)),
        pl.BlockSpec(dw.shape, const),
        pl.BlockSpec((1, d), const), pl.BlockSpec((1, d), const), pl.BlockSpec((1, d), const),
        pl.BlockSpec((d, d), const), pl.BlockSpec((d, d), const), pl.BlockSpec((d, d), const),
    ]
    return pl.pallas_call(
        functools.partial(_merge_kernel, tm=tm, tps=tps, rc=64),
        grid=(m // tm,),
        in_specs=in_specs,
        out_specs=pl.BlockSpec((tm, d), row),
        out_shape=jax.ShapeDtypeStruct((m, d), F32),
        scratch_shapes=[pltpu.VMEM((tm + 2 * CONV_HALO, d), F32), pltpu.VMEM((tm, d), F32)],
        compiler_params=_cparams(("parallel",)),
        name="conv_merge",
    )(x, ao, u, u, u, gates, mod3, dw, dwb, lng, lnb, wco, wao, wout)


def _ffn_kernel(x_ref, mod_ref, g_ref, wg_ref, wu_ref, wd_ref, o_ref, h_ref, acc_ref, *, nf):
    f = pl.program_id(1)

    @pl.when(f == 0)
    def _():
        h = _norm_modulate(x_ref[...], g_ref[...], mod_ref[:, 3 * D_MODEL:4 * D_MODEL],
                           mod_ref[:, 4 * D_MODEL:5 * D_MODEL])
        h_ref[...] = h.astype(BF16)
        acc_ref[...] = jnp.zeros(acc_ref.shape, F32)

    hb = h_ref[...]
    a = jnp.dot(hb, wg_ref[...], preferred_element_type=F32)
    b = jnp.dot(hb, wu_ref[...], preferred_element_type=F32)
    acc_ref[...] += jnp.dot((_silu(a) * b).astype(BF16), wd_ref[...], preferred_element_type=F32)

    @pl.when(f == nf - 1)
    def _():
        o_ref[...] = x_ref[...] + mod_ref[:, 5 * D_MODEL:6 * D_MODEL] * acc_ref[...]


def _dense_ffn(x, mod3, row_of, g, wg, wu, wd, tm, tf):
    m, d = x.shape
    dff = wg.shape[1]
    nf = dff // tf
    return pl.pallas_call(
        functools.partial(_ffn_kernel, nf=nf),
        grid=(m // tm, nf),
        in_specs=[pl.BlockSpec((tm, d), lambda i, f: (i, 0)),
                  pl.BlockSpec((None, 1, mod3.shape[2]), lambda i, f: (row_of(i), 0, 0)),
                  pl.BlockSpec((1, d), lambda i, f: (0, 0)),
                  pl.BlockSpec((d, tf), lambda i, f: (0, f)),
                  pl.BlockSpec((d, tf), lambda i, f: (0, f)),
                  pl.BlockSpec((tf, d), lambda i, f: (f, 0))],
        out_specs=pl.BlockSpec((tm, d), lambda i, f: (i, 0)),
        out_shape=jax.ShapeDtypeStruct((m, d), F32),
        scratch_shapes=[pltpu.VMEM((tm, d), BF16), pltpu.VMEM((tm, d), F32)],
        compiler_params=_cparams(("parallel", "arbitrary")),
        name="dense_ffn",
    )(x, mod3, g, wg, wu, wd)


R_E0, R_E1, R_RANK0, R_RANK1, R_W0, R_W1 = range(6)
ITEM_VALID, ITEM_FIRST, ITEM_LAST, ITEM_ZERO = 1, 2, 4, 8


def _router_kernel(x_ref, mod_ref, g_ref, wr_ref, h_ref, route_ref, cum_ref, base_ref):
    @pl.when(pl.program_id(0) == 0)
    def _():
        base_ref[...] = jnp.zeros(base_ref.shape, F32)

    h = _norm_modulate(x_ref[...], g_ref[...], mod_ref[:, 3 * D_MODEL:4 * D_MODEL],
                       mod_ref[:, 4 * D_MODEL:5 * D_MODEL])
    h_ref[...] = h.astype(BF16)
    logits = jnp.dot(h, wr_ref[...], preferred_element_type=F32, precision=lax.Precision.HIGHEST)
    tb = logits.shape[0]
    lane = lax.broadcasted_iota(jnp.int32, logits.shape, 1)
    lg = jnp.where(lane < N_EXPERTS, logits, NEG_BIG)
    m1 = jnp.max(lg, axis=1, keepdims=True)
    e0 = jnp.min(jnp.where(lg == m1, lane, LANES), axis=1, keepdims=True)
    lg2 = jnp.where(lane == e0, NEG_BIG, lg)
    m2 = jnp.max(lg2, axis=1, keepdims=True)
    e1 = jnp.min(jnp.where(lg2 == m2, lane, LANES), axis=1, keepdims=True)
    ex = jnp.exp(m2 - m1)
    w0 = 1.0 / (1.0 + ex)
    w1 = ex / (1.0 + ex)
    oh0 = lane == e0
    oh1 = lane == e1
    oh = jnp.where(oh0, 1.0, 0.0) + jnp.where(oh1, 1.0, 0.0)
    ltri = jnp.where(lax.broadcasted_iota(jnp.int32, (tb, tb), 1)
                     < lax.broadcasted_iota(jnp.int32, (tb, tb), 0), 1.0, 0.0).astype(BF16)
    rank = jnp.dot(ltri, oh.astype(BF16), preferred_element_type=F32) + base_ref[...]
    rank0 = jnp.sum(jnp.where(oh0, rank, 0.0), axis=1, keepdims=True)
    rank1 = jnp.sum(jnp.where(oh1, rank, 0.0), axis=1, keepdims=True)
    route = jnp.zeros(logits.shape, F32)
    for j, val in ((R_E0, e0.astype(F32)), (R_E1, e1.astype(F32)), (R_RANK0, rank0),
                   (R_RANK1, rank1), (R_W0, w0), (R_W1, w1)):
        route = jnp.where(lane == j, val, route)
    route_ref[...] = route
    base = base_ref[...] + jnp.sum(oh, axis=0, keepdims=True)
    base_ref[...] = base
    cum_ref[...] = base


def _router(x, mod3, row_of, g, wr_pad, tb):
    m, d = x.shape
    nb = m // tb
    return pl.pallas_call(
        _router_kernel,
        grid=(nb,),
        in_specs=[pl.BlockSpec((tb, d), lambda i: (i, 0)),
                  pl.BlockSpec((None, 1, mod3.shape[2]), lambda i: (row_of(i), 0, 0)),
                  pl.BlockSpec((1, d), lambda i: (0, 0)),
                  pl.BlockSpec((d, LANES), lambda i: (0, 0))],
        out_specs=[pl.BlockSpec((tb, d), lambda i: (i, 0)),
                   pl.BlockSpec((tb, LANES), lambda i: (i, 0)),
                   pl.BlockSpec((None, 1, LANES), lambda i: (i, 0, 0))],
        out_shape=[jax.ShapeDtypeStruct((m, d), BF16),
                   jax.ShapeDtypeStruct((m, LANES), F32),
                   jax.ShapeDtypeStruct((nb, 1, LANES), F32)],
        scratch_shapes=[pltpu.VMEM((1, LANES), F32)],
        compiler_params=_cparams(("arbitrary",)),
        name="moe_router",
    )(x, mod3, g, wr_pad)


def _moe_plan(route, cum, nb, tb, tw):
    i32 = jnp.int32
    wpt = tb // tw
    span = wpt + 1
    n_tiles = 2 * nb + N_EXPERTS
    n_items = n_tiles * wpt + N_EXPERTS * nb
    cum_in = cum[:, 0, :N_EXPERTS].astype(i32)
    cum_ex = jnp.concatenate([jnp.zeros((1, N_EXPERTS), i32), cum_in[:-1]], axis=0)
    counts = cum_in[-1]
    ntile = (counts + tb - 1) // tb
    tile_end = jnp.cumsum(ntile)
    start = (tile_end - ntile) * tb
    n_used = tile_end[-1]
    t_idx = jnp.arange(n_tiles, dtype=i32)
    tile_valid = (t_idx < n_used).astype(i32)
    tile_src = jnp.minimum(t_idx, n_used - 1)
    tile_expert = jnp.minimum(jnp.sum((tile_src[:, None] >= tile_end[None, :]).astype(i32), axis=1),
                              N_EXPERTS - 1)
    e0 = route[:, R_E0].astype(i32)
    e1 = route[:, R_E1].astype(i32)
    pos0 = start[e0] + route[:, R_RANK0].astype(i32)
    pos1 = start[e1] + route[:, R_RANK1].astype(i32)
    lo = start[None, :] + cum_ex
    hi = start[None, :] + cum_in
    w_lo = lo // tw
    npair = jnp.where(hi > lo, (hi - 1) // tw - w_lo + 1, 0)
    j = jnp.arange(span, dtype=i32)
    c_valid = j[None, None, :] < npair[:, :, None]
    c_win = w_lo[:, :, None] + j[None, None, :]
    c_blk = jnp.broadcast_to(jnp.arange(nb, dtype=i32)[:, None, None], c_valid.shape)
    jp = jnp.arange(wpt - 1, dtype=i32)
    p_win = ((start + counts + tw - 1) // tw)[:, None] + jp[None, :]
    p_valid = p_win < (tile_end * wpt)[:, None]

    def compact(valid, cols, n):
        order = jnp.argsort(jnp.logical_not(valid), stable=True)[:n]
        n_valid = jnp.sum(valid.astype(i32))
        k = jnp.minimum(jnp.arange(n, dtype=i32), n_valid - 1)
        return [c[order][k] for c in cols], jnp.arange(n, dtype=i32) < n_valid

    def flags(valid, key, extra):
        prev = jnp.concatenate([key[:1] - 1, key[:-1]])
        nxt = jnp.concatenate([key[1:], key[-1:] - 1])
        last_valid = jnp.concatenate([jnp.logical_not(valid[1:]), jnp.ones((1,), bool)])
        return (valid.astype(i32) * ITEM_VALID
                + (valid & (key != prev)).astype(i32) * ITEM_FIRST
                + (valid & ((key != nxt) | last_valid)).astype(i32) * ITEM_LAST
                + (valid & extra).astype(i32) * ITEM_ZERO)

    em = lambda a: jnp.transpose(a, (1, 0, 2)).reshape(-1)
    bm = lambda a: a.reshape(-1)
    n_pad = N_EXPERTS * (wpt - 1)
    g_valid = jnp.concatenate([em(c_valid), p_valid.reshape(-1)])
    g_win = jnp.concatenate([em(c_win), p_win.reshape(-1)])
    g_blk = jnp.concatenate([em(c_blk), jnp.zeros((n_pad,), i32)])
    g_zero = jnp.concatenate([jnp.zeros(em(c_valid).shape, bool), jnp.ones((n_pad,), bool)])
    (g_win, g_blk, g_zero), g_ok = compact(g_valid, [g_win, g_blk, g_zero], n_items + n_pad)
    (c_win2, c_blk2), c_ok = compact(bm(c_valid), [bm(c_win), bm(c_blk)], n_items)
    return dict(pos0=pos0, pos1=pos1, start=start, tile_src=tile_src, tile_expert=tile_expert,
                tile_valid=tile_valid, n_tiles=n_tiles,
                gather=(g_win, g_blk, flags(g_ok, g_win, g_zero)),
                combine=(c_blk2, c_win2, flags(c_ok, c_blk2, jnp.zeros_like(c_ok))))


def _gather_kernel(it_win, it_blk, it_flag, p0_ref, p1_ref, h_ref, o_ref, *, tb, tw):
    k = pl.program_id(0)
    flag = it_flag[k]

    @pl.when((flag & ITEM_ZERO) != 0)
    def _():
        o_ref[...] = jnp.zeros(o_ref.shape, BF16)

    @pl.when(((flag & ITEM_VALID) != 0) & ((flag & ITEM_ZERO) == 0))
    def _():
        rows = lax.broadcasted_iota(jnp.int32, (tw, tb), 0) + it_win[k] * tw
        sel = jnp.where(p0_ref[...] == rows, 1.0, 0.0) + jnp.where(p1_ref[...] == rows, 1.0, 0.0)
        c = jnp.dot(sel.astype(BF16), h_ref[...], preferred_element_type=F32)

        @pl.when((flag & ITEM_FIRST) != 0)
        def _():
            o_ref[...] = c.astype(BF16)

        @pl.when((flag & ITEM_FIRST) == 0)
        def _():
            o_ref[...] = (o_ref[...].astype(F32) + c).astype(BF16)


def _moe_gather(h, plan, tb, tw):
    m, d = h.shape
    nb = m // tb
    it_win, it_blk, it_flag = plan["gather"]
    p0 = plan["pos0"].reshape(nb, 1, tb)
    p1 = plan["pos1"].reshape(nb, 1, tb)
    grid_spec = pltpu.PrefetchScalarGridSpec(
        num_scalar_prefetch=3,
        grid=(it_win.shape[0],),
        in_specs=[pl.BlockSpec((None, 1, tb), lambda k, w, b, f: (b[k], 0, 0)),
                  pl.BlockSpec((None, 1, tb), lambda k, w, b, f: (b[k], 0, 0)),
                  pl.BlockSpec((tb, d), lambda k, w, b, f: (b[k], 0))],
        out_specs=pl.BlockSpec((tw, d), lambda k, w, b, f: (w[k], 0)),
    )
    return pl.pallas_call(
        functools.partial(_gather_kernel, tb=tb, tw=tw),
        grid_spec=grid_spec,
        out_shape=jax.ShapeDtypeStruct((plan["n_tiles"] * tb, d), BF16),
        compiler_params=_cparams(("arbitrary",)),
        name="moe_gather",
    )(it_win, it_blk, it_flag, p0, p1, h)


def _expert_kernel(ts_ref, te_ref, tv_ref, x_ref, wg_ref, wu_ref, wd_ref, o_ref, acc_ref, *, nf):
    t = pl.program_id(0)
    f = pl.program_id(1)

    @pl.when(tv_ref[t] != 0)
    def _():
        xb = x_ref[...]
        a = jnp.dot(xb, wg_ref[...], preferred_element_type=F32)
        b = jnp.dot(xb, wu_ref[...], preferred_element_type=F32)
        y = jnp.dot((_silu(a) * b).astype(BF16), wd_ref[...], preferred_element_type=F32)

        @pl.when(f == 0)
        def _():
            acc_ref[...] = y

        @pl.when(f != 0)
        def _():
            acc_ref[...] += y

        @pl.when(f == nf - 1)
        def _():
            o_ref[...] = acc_ref[...].astype(BF16)


def _moe_experts(xs, plan, wg, wu, wd, tb, tf):
    d = xs.shape[1]
    nf = wg.shape[2] // tf
    fsel = lambda f, tv, t: jnp.where(tv[t] != 0, f, nf - 1)
    grid_spec = pltpu.PrefetchScalarGridSpec(
        num_scalar_prefetch=3,
        grid=(plan["n_tiles"], nf),
        in_specs=[pl.BlockSpec((tb, d), lambda t, f, ts, te, tv: (ts[t], 0)),
                  pl.BlockSpec((None, d, tf), lambda t, f, ts, te, tv: (te[t], 0, fsel(f, tv, t))),
                  pl.BlockSpec((None, d, tf), lambda t, f, ts, te, tv: (te[t], 0, fsel(f, tv, t))),
                  pl.BlockSpec((None, tf, d), lambda t, f, ts, te, tv: (te[t], fsel(f, tv, t), 0))],
        out_specs=pl.BlockSpec((tb, d), lambda t, f, ts, te, tv: (ts[t], 0)),
        scratch_shapes=[pltpu.VMEM((tb, d), F32)],
    )
    return pl.pallas_call(
        functools.partial(_expert_kernel, nf=nf),
        grid_spec=grid_spec,
        out_shape=jax.ShapeDtypeStruct(xs.shape, BF16),
        compiler_params=_cparams(("arbitrary", "arbitrary")),
        name="moe_experts",
    )(plan["tile_src"], plan["tile_expert"], plan["tile_valid"], xs, wg, wu, wd)


def _combine_kernel(it_blk, it_win, it_flag, x_ref, mod_ref, route_ref, start_ref, y_ref,
                    o_ref, acc_ref, *, tb, tw):
    k = pl.program_id(0)
    flag = it_flag[k]

    @pl.when((flag & ITEM_VALID) != 0)
    def _():
        route = route_ref[...]
        lanef = lax.broadcasted_iota(jnp.int32, route.shape, 1).astype(F32)
        start = start_ref[...]
        base = (it_win[k] * tw).astype(F32)

        def slot(je, jr, jw):
            e = route[:, je:je + 1]
            s = jnp.sum(jnp.where(lanef == e, start, 0.0), axis=1, keepdims=True)
            return s + route[:, jr:jr + 1] - base, route[:, jw:jw + 1]

        pos0, w0 = slot(R_E0, R_RANK0, R_W0)
        pos1, w1 = slot(R_E1, R_RANK1, R_W1)
        col = lax.broadcasted_iota(jnp.int32, (tb, tw), 1).astype(F32)
        sel = jnp.where(col == pos0, w0, 0.0) + jnp.where(col == pos1, w1, 0.0)
        c = jnp.dot(sel.astype(BF16), y_ref[...], preferred_element_type=F32)

        @pl.when((flag & ITEM_FIRST) != 0)
        def _():
            acc_ref[...] = c

        @pl.when((flag & ITEM_FIRST) == 0)
        def _():
            acc_ref[...] += c

        @pl.when((flag & ITEM_LAST) != 0)
        def _():
            o_ref[...] = x_ref[...] + mod_ref[:, 5 * D_MODEL:6 * D_MODEL] * acc_ref[...]


def _moe_combine(x, mod3, row_of, route, ys, plan, tb, tw):
    m, d = x.shape
    it_blk, it_win, it_flag = plan["combine"]
    start = jnp.zeros((1, LANES), F32).at[0, :N_EXPERTS].set(plan["start"].astype(F32))
    grid_spec = pltpu.PrefetchScalarGridSpec(
        num_scalar_prefetch=3,
        grid=(it_blk.shape[0],),
        in_specs=[pl.BlockSpec((tb, d), lambda k, b, w, f: (b[k], 0)),
                  pl.BlockSpec((None, 1, mod3.shape[2]), lambda k, b, w, f: (row_of(b[k]), 0, 0)),
                  pl.BlockSpec((tb, LANES), lambda k, b, w, f: (b[k], 0)),
                  pl.BlockSpec((1, LANES), lambda k, b, w, f: (0, 0)),
                  pl.BlockSpec((tw, d), lambda k, b, w, f: (w[k], 0))],
        out_specs=pl.BlockSpec((tb, d), lambda k, b, w, f: (b[k], 0)),
        scratch_shapes=[pltpu.VMEM((tb, d), F32)],
    )
    return pl.pallas_call(
        functools.partial(_combine_kernel, tb=tb, tw=tw),
        grid_spec=grid_spec,
        out_shape=jax.ShapeDtypeStruct((m, d), F32),
        compiler_params=_cparams(("arbitrary",)),
        name="moe_combine",
    )(it_blk, it_win, it_flag, x, mod3, route, start, ys)


def _moe_ffn(x, mod3, row_of, g, wr_pad, wg, wu, wd, tb, tf):
    nb = x.shape[0] // tb
    tw = min(MOE_WINDOW, tb)
    h, route, cum = _router(x, mod3, row_of, g, wr_pad, tb)
    plan = _moe_plan(route, cum, nb, tb, tw)
    xs = _moe_gather(h, plan, tb, tw)
    ys = _moe_experts(xs, plan, wg, wu, wd, tb, tf)
    return _moe_combine(x, mod3, row_of, route, ys, plan, tb, tw)


def _rope_tables(seq):
    t = jnp.arange(seq, dtype=jnp.int32)
    row = (t // GRID_W).astype(F32)
    col = (t % GRID_W).astype(F32)
    inv = ROPE_THETA ** (-jnp.arange(0, ROT_AXIS, 2, dtype=F32) / ROT_AXIS)
    ar = row[:, None] * inv
    ac = col[:, None] * inv
    ang = jnp.concatenate([ar, ar, ac, ac], axis=-1)
    ang = jnp.concatenate([ang, ang], axis=-1)
    cos, sin = jnp.cos(ang), jnp.sin(ang)
    upper = (jnp.arange(HEAD_W) & ROT_HALF) != 0
    sa = jnp.where(upper, 0.0, -sin)
    sb = jnp.where(upper, sin, 0.0)
    return cos, sa, sb


def _pick_tile(n, pref):
    t = min(n, pref)
    while n % t:
        t //= 2
    return t


def kernel(x, c, ctx, c_ctx, w_mod, b_mod, g_mix, w_in, q_norm_g, k_norm_g, lambda_q1, lambda_k1, lambda_q2, lambda_k2, subln_g, w_attn_o, dw_weight, dw_bias, conv_ln_g, conv_ln_b, w_conv_o, w_out, g_ffn, w_ff_gate, w_ff_up, w_ff_down, w_router, w_exp_gate, w_exp_up, w_exp_down):
    n_batch, seq, d = x.shape
    ctx_len = ctx.shape[1]
    depth = w_mod.shape[0]
    assert d == D_MODEL and seq % GRID_W == 0

    tm_l = _pick_tile(seq, 512)
    tm_c = _pick_tile(ctx_len, 512)
    ctx_row = n_batch
    mod_rows = -(-(n_batch + 1) // 8) * 8
    cvec = jnp.zeros((mod_rows, d), F32).at[:n_batch].set(c).at[ctx_row].set(c_ctx)

    cos, sa, sb = _rope_tables(seq)
    gid = jnp.arange(HEAD_W) // HEAD_DIM
    gsum = ((gid[:, None] == gid[None, :]).astype(F32) / HEAD_DIM).astype(BF16)

    def rope_with_gain(g_row):
        return (cos * g_row, sa * jnp.roll(g_row, -ROT_HALF, axis=1),
                sb * jnp.roll(g_row, ROT_HALF, axis=1))

    xl = x.reshape(n_batch * seq, d)
    xc = ctx.reshape(n_batch * ctx_len, d)
    lat_row = lambda i: (i * tm_l) // seq
    ctx_row_of = lambda i: ctx_row

    for l in range(depth):
        last = l == depth - 1
        lam_init = 0.8 - 0.6 * math.exp(-0.3 * l)
        mod3 = _modulation(cvec, w_mod[l], b_mod[l]).reshape(mod_rows, 1, 6 * d)
        w_bf = w_in[l].astype(BF16)
        gmix = g_mix[l].reshape(1, d)
        qg = jnp.tile(q_norm_g[l], 2).reshape(1, HEAD_W) * Q_SCALE
        kg = jnp.tile(k_norm_g[l], 2).reshape(1, HEAD_W)
        tabs = rope_with_gain(qg) + rope_with_gain(kg)
        lamv = jnp.stack([lambda_q1[l], lambda_k1[l], lambda_q2[l], lambda_k2[l]])
        sg = subln_g[l].reshape(HEAD_W, 1)
        dw = jnp.zeros((32, d), F32).at[:CONV_K].set(dw_weight[l])
        merge_w = (dw, dw_bias[l].reshape(1, d), conv_ln_g[l].reshape(1, d),
                   conv_ln_b[l].reshape(1, d), w_conv_o[l].astype(BF16),
                   w_attn_o[l].astype(BF16), w_out[l].astype(BF16))

        q, k, v, u, gates = _in_proj_full(xl, mod3, lat_row, gmix, w_bf, qg, kg, gsum, tabs,
                                          seq, tm_l)
        if last:
            _, kc, vc = _proj(xc, mod3, ctx_row_of, gmix, w_bf, qg, kg, gsum, None,
                              ctx_len, GROUPS_KV, True, tm_c)
        else:
            qc, kc, vc, uc, gatesc = _in_proj_full(xc, mod3, ctx_row_of, gmix, w_bf, qg, kg,
                                                   gsum, None, ctx_len, tm_c)
        tk = ATTN_TK
        ao = _diff_attention(q, [(k, v), (kc, vc)], lamv, sg, n_batch, seq,
                             _pick_tile(seq, ATTN_TQ), tk, lam_init)
        xl = _merge(xl, ao, u, gates, mod3, lat_row, *merge_w, seq, tm_l)
        if not last:
            aoc = _diff_attention(qc, [(kc, vc)], lamv, sg, n_batch, ctx_len,
                                  _pick_tile(ctx_len, ATTN_TQ), tk, lam_init)
            xc = _merge(xc, aoc, uc, gatesc, mod3, ctx_row_of, *merge_w, ctx_len, tm_c)

        gf = g_ffn[l].reshape(1, d)
        i = l // 2
        if l % 2 == 0:
            wts = (w_ff_gate[i].astype(BF16), w_ff_up[i].astype(BF16), w_ff_down[i].astype(BF16))
            tf = _pick_ff_tile(wts[0].shape[1])
            xl = _dense_ffn(xl, mod3, lat_row, gf, *wts, tm_l, tf)
            if not last:
                xc = _dense_ffn(xc, mod3, ctx_row_of, gf, *wts, tm_c, tf)
        else:
            wr_pad = jnp.zeros((d, LANES), F32).at[:, :N_EXPERTS].set(w_router[i])
            wts = (w_exp_gate[i].astype(BF16), w_exp_up[i].astype(BF16),
                   w_exp_down[i].astype(BF16))
            tf = _pick_ff_tile(wts[0].shape[2])
            xl = _moe_ffn(xl, mod3, lat_row, gf, wr_pad, *wts, tm_l, tf)
            if not last:
                xc = _moe_ffn(xc, mod3, ctx_row_of, gf, wr_pad, *wts, tm_c, tf)
    return xl.reshape(n_batch, seq, d)


def _pick_ff_tile(dff):
    best = LANES
    for t in range(LANES, 1792 + 1, LANES):
        if dff % t == 0:
            best = t
    return best
```

```python
import functools
import math

import jax
import jax.numpy as jnp
from jax import lax
from jax.experimental import pallas as pl
from jax.experimental.pallas import tpu as pltpu

F32 = jnp.float32
BF16 = jnp.bfloat16

LANES = 128
D_MODEL = 1024
N_HEADS = 8
HEAD_DIM = 64
HEAD_W = 2 * HEAD_DIM
GRID_W = 64
ROT_AXIS = HEAD_DIM // 2
ROT_HALF = ROT_AXIS // 2
ROPE_THETA = 10000.0
CONV_K = 31
CONV_HALO = 16
N_EXPERTS = 8
EPS = 1e-6
N_GROUPS = 7
ATTN_TQ = 1024
ATTN_TK = 512
MOE_WINDOW = 256
KNORM_ROWS = 512
BOUND_SLACK = 1.01
MIN_DENOM = 2.0 ** -60
VMEM_LIMIT = 52 * 1024 * 1024
Q_SCALE = (HEAD_DIM ** -0.5) * math.log2(math.e)
NEG_BIG = -1e30


def _cparams(sem):
    return pltpu.CompilerParams(dimension_semantics=sem, vmem_limit_bytes=VMEM_LIMIT)


def _sigmoid(v):
    return 0.5 * jnp.tanh(0.5 * v) + 0.5


def _silu(v):
    return v * _sigmoid(v)


def _norm_modulate(xf, g, shift, scale):
    ms = jnp.mean(xf * xf, axis=-1, keepdims=True)
    return (xf * lax.rsqrt(ms + EPS) * g) * (1.0 + scale) + shift


def _mod_kernel(c_ref, w_ref, b_ref, o_ref):
    a = _silu(c_ref[...])
    o_ref[...] = jnp.dot(a, w_ref[...], preferred_element_type=F32,
                         precision=lax.Precision.HIGHEST) + b_ref[...]


def _modulation(cvec, w, b):
    rows, d = cvec.shape
    n = w.shape[1]
    tn = n // 4
    return pl.pallas_call(
        _mod_kernel,
        grid=(n // tn,),
        in_specs=[pl.BlockSpec((rows, d), lambda j: (0, 0)),
                  pl.BlockSpec((d, tn), lambda j: (0, j)),
                  pl.BlockSpec((1, tn), lambda j: (0, j))],
        out_specs=pl.BlockSpec((rows, tn), lambda j: (0, j)),
        out_shape=jax.ShapeDtypeStruct((rows, n), F32),
        compiler_params=_cparams(("arbitrary",)),
        name="modulation",
    )(cvec, w, b.reshape(1, n))


G_Q, G_K, G_V, G_GLU_A, G_GLU_G, G_GATE_A, G_GATE_C = range(N_GROUPS)
GROUPS_QK = (G_Q, G_K)
GROUPS_REST = (G_V, G_GLU_A, G_GLU_G, G_GATE_A, G_GATE_C)
GROUPS_KV = (G_K, G_V)


def _proj_kernel(*refs, groups, from_x, rope):
    if from_x:
        x_ref, mod_ref, gmix_ref = refs[:3]
        pos = 3
    else:
        hin_ref = refs[0]
        pos = 1
    w_refs = refs[pos:pos + len(groups)]
    pos += len(groups)
    qg_ref, kg_ref, gsum_ref = refs[pos:pos + 3]
    pos += 3
    tabs = {}
    if rope:
        for g in (G_Q, G_K):
            if g in groups:
                tabs[g] = refs[pos:pos + 3]
                pos += 3
    outs = list(refs[pos:])
    if from_x:
        h = _norm_modulate(x_ref[...], gmix_ref[...], mod_ref[:, 0:D_MODEL],
                           mod_ref[:, D_MODEL:2 * D_MODEL]).astype(BF16)
        h_out = outs.pop(0)
        h_out[...] = h
    else:
        h = hin_ref[...]
    out_of = {}
    for g in groups:
        if g in (G_Q, G_K, G_V):
            out_of[g] = outs.pop(0)
    if G_GLU_A in groups:
        u_ref = outs.pop(0)
    if G_GATE_A in groups:
        gates_ref = outs.pop(0)

    def qk_norm(acc, out_ref, g_row_ref, tables):
        gsum = gsum_ref[...]
        for hb in range(N_HEADS):
            sl = slice(hb * HEAD_W, (hb + 1) * HEAD_W)
            blk = acc[:, sl]
            ms = jnp.dot((blk * blk).astype(BF16), gsum, preferred_element_type=F32)
            y = blk * lax.rsqrt(ms + EPS)
            if tables is not None:
                c_ref, sa_ref, sb_ref = tables
                y = (y * c_ref[...] + pltpu.roll(y, HEAD_W - ROT_HALF, 1) * sa_ref[...]
                     + pltpu.roll(y, ROT_HALF, 1) * sb_ref[...])
            else:
                y = y * g_row_ref[...]
            out_ref[:, sl] = y.astype(BF16)

    glu_a = None
    for g, w_ref in zip(groups, w_refs):
        acc = jnp.dot(h, w_ref[...], preferred_element_type=F32)
        if g == G_Q:
            qk_norm(acc, out_of[g], qg_ref, tabs.get(g))
        elif g == G_K:
            qk_norm(acc, out_of[g], kg_ref, tabs.get(g))
        elif g == G_V:
            out_of[g][...] = acc.T.astype(BF16)
        elif g == G_GLU_A:
            glu_a = acc
        elif g == G_GLU_G:
            u_ref[...] = glu_a * _sigmoid(acc)
        elif g == G_GATE_A:
            gates_ref[:, 0:D_MODEL] = _sigmoid(acc).astype(BF16)
        elif g == G_GATE_C:
            gates_ref[:, D_MODEL:2 * D_MODEL] = _sigmoid(acc).astype(BF16)


def _proj(src, mod3, row_of, gmix, w_bf, qg, kg, gsum, rope_tabs, seq, groups, from_x, tm):
    m, d = src.shape
    tps = seq // tm
    rope = rope_tabs is not None
    const = lambda i: (0, 0)
    row = lambda i: (i, 0)
    if from_x:
        in_specs = [pl.BlockSpec((tm, d), row),
                    pl.BlockSpec((None, 1, mod3.shape[2]), lambda i: (row_of(i), 0, 0)),
                    pl.BlockSpec((1, d), const)]
        args = [src, mod3, gmix]
    else:
        in_specs = [pl.BlockSpec((tm, d), row)]
        args = [src]
    for g in groups:
        in_specs.append(pl.BlockSpec((d, D_MODEL), lambda i, g=g: (0, g)))
        args.append(w_bf)
    in_specs += [pl.BlockSpec((1, HEAD_W), const), pl.BlockSpec((1, HEAD_W), const),
                 pl.BlockSpec((HEAD_W, HEAD_W), const)]
    args += [qg, kg, gsum]
    if rope:
        for g, t3 in ((G_Q, rope_tabs[0:3]), (G_K, rope_tabs[3:6])):
            if g in groups:
                for t in t3:
                    in_specs.append(pl.BlockSpec((tm, HEAD_W), lambda i: (i % tps, 0)))
                    args.append(t)
    row_spec = pl.BlockSpec((tm, D_MODEL), row)
    rows_bf = jax.ShapeDtypeStruct((m, D_MODEL), BF16)
    out_specs, out_shape = [], []
    if from_x:
        out_specs.append(row_spec)
        out_shape.append(rows_bf)
    for g in groups:
        if g in (G_Q, G_K):
            out_specs.append(row_spec)
            out_shape.append(rows_bf)
        elif g == G_V:
            out_specs.append(pl.BlockSpec((D_MODEL, tm), lambda i: (i // tps, i % tps)))
            out_shape.append(jax.ShapeDtypeStruct(((m // seq) * D_MODEL, seq), BF16))
    if G_GLU_A in groups:
        out_specs.append(row_spec)
        out_shape.append(jax.ShapeDtypeStruct((m, D_MODEL), F32))
    if G_GATE_A in groups:
        out_specs.append(pl.BlockSpec((tm, 2 * D_MODEL), row))
        out_shape.append(jax.ShapeDtypeStruct((m, 2 * D_MODEL), BF16))
    return pl.pallas_call(
        functools.partial(_proj_kernel, groups=groups, from_x=from_x, rope=rope),
        grid=(m // tm,),
        in_specs=in_specs,
        out_specs=out_specs,
        out_shape=out_shape,
        compiler_params=_cparams(("parallel",)),
        name="in_proj",
    )(*args)


def _in_proj_full(x, mod3, row_of, gmix, w_bf, qg, kg, gsum, rope_tabs, seq, tm):
    h, q, k = _proj(x, mod3, row_of, gmix, w_bf, qg, kg, gsum, rope_tabs, seq, GROUPS_QK, True, tm)
    vt, u, gates = _proj(h, mod3, row_of, gmix, w_bf, qg, kg, gsum, None, seq, GROUPS_REST,
                         False, tm)
    return q, k, vt, u, gates


def _attn_kernel(*refs, src_lens, tq, tk, lam_init):
    n_src = len(src_lens)
    q_ref = refs[0]
    kv_refs = refs[1:1 + 2 * n_src]
    lamv_ref, sg_ref, ksum_ref, o_ref, acc_ref, kmax_ref = refs[1 + 2 * n_src:7 + 2 * n_src]
    s_refs = refs[7 + 2 * n_src:9 + 2 * n_src]
    n_cb = 2 * tq // LANES

    @pl.when(pl.program_id(2) == 0)
    def _():
        best = jnp.zeros((1, HEAD_W), F32)
        for si in range(n_src):
            for off in range(0, src_lens[si], KNORM_ROWS):
                rows = min(KNORM_ROWS, src_lens[si] - off)
                kf = kv_refs[2 * si][off:off + rows, :].astype(F32)
                n2 = jnp.dot((kf * kf).astype(BF16), ksum_ref[...], preferred_element_type=F32)
                best = jnp.maximum(best, jnp.max(n2, axis=0, keepdims=True))
        kmax_ref[...] = jnp.sqrt(best) * BOUND_SLACK

    q = q_ref[...]
    lane = lax.broadcasted_iota(jnp.int32, q.shape, 1)
    zero = jnp.zeros_like(q)
    q2 = jnp.concatenate([jnp.where(lane < HEAD_DIM, q, zero),
                          jnp.where(lane >= HEAD_DIM, q, zero)], axis=0)
    chunks = [(si, off, min(tk, src_lens[si])) for si in range(n_src)
              for off in range(0, src_lens[si], min(tk, src_lens[si]))]

    def scores(si, off, tkk):
        kc = kv_refs[2 * si][pl.ds(off, tkk), :]
        return lax.dot_general(kc, q2, (((1,), (1,)), ((), ())), preferred_element_type=F32)

    def finalize(acc, l):
        lv = lamv_ref[...]
        lam = (jnp.exp(jnp.sum(lv[0:1] * lv[1:2], axis=1, keepdims=True))
               - jnp.exp(jnp.sum(lv[2:3] * lv[3:4], axis=1, keepdims=True)) + lam_init)
        inv = 1.0 / l
        o_t = acc[:, 0:tq] * inv[:, 0:tq] - lam * (acc[:, tq:2 * tq] * inv[:, tq:2 * tq])
        ms = jnp.mean(o_t * o_t, axis=0, keepdims=True)
        y_t = o_t * lax.rsqrt(ms + EPS) * (sg_ref[...] * (1.0 - lam_init))
        o_ref[...] = y_t.T.astype(BF16)

    q2f = q2.astype(F32)
    qn2 = lax.dot_general(jnp.ones((8, HEAD_W), BF16), (q2f * q2f).astype(BF16),
                          (((1,), (1,)), ((), ())), preferred_element_type=F32)[0:1, :]
    kmax = kmax_ref[...]
    kmax_row = jnp.concatenate([jnp.broadcast_to(kmax[:, 0:1], (1, tq)),
                                jnp.broadcast_to(kmax[:, HEAD_DIM:HEAD_DIM + 1], (1, tq))], axis=1)
    bound = jnp.sqrt(qn2) * BOUND_SLACK * kmax_row
    s_refs[0][0:chunks[0][2], :] = scores(*chunks[0])
    l = jnp.zeros((1, 2 * tq), F32)
    for ci, (si, off, tkk) in enumerate(chunks):
        if ci + 1 < len(chunks):
            s_refs[(ci + 1) % 2][0:chunks[ci + 1][2], :] = scores(*chunks[ci + 1])
        p = jnp.exp2(s_refs[ci % 2][0:tkk, :] - bound)
        l = l + jnp.sum(p, axis=0, keepdims=True)
        pv = jnp.dot(kv_refs[2 * si + 1][:, pl.ds(off, tkk)], p.astype(BF16),
                     preferred_element_type=F32)
        if ci == 0:
            acc_ref[...] = pv
        else:
            acc_ref[...] += pv
    healthy = jnp.min(l) >= MIN_DENOM

    @pl.when(healthy)
    def _():
        finalize(acc_ref[...], l)

    @pl.when(jnp.logical_not(healthy))
    def _():
        acc_ref[...] = jnp.zeros(acc_ref.shape, F32)
        m = [jnp.full((1, LANES), NEG_BIG, F32) for _ in range(n_cb)]
        lx = [jnp.zeros((1, LANES), F32) for _ in range(n_cb)]
        for si, off, tkk in chunks:
            s = scores(si, off, tkk)
            alphas, ps = [], []
            for cb in range(n_cb):
                s_cb = s[:, cb * LANES:(cb + 1) * LANES]
                m_new = jnp.maximum(m[cb], jnp.max(s_cb, axis=0, keepdims=True))
                alpha = jnp.exp2(m[cb] - m_new)
                p_cb = jnp.exp2(s_cb - m_new)
                lx[cb] = alpha * lx[cb] + jnp.sum(p_cb, axis=0, keepdims=True)
                alphas.append(alpha)
                ps.append(p_cb.astype(BF16))
                m[cb] = m_new
            acc_ref[...] = (jnp.concatenate(alphas, axis=1) * acc_ref[...]
                            + jnp.dot(kv_refs[2 * si + 1][:, pl.ds(off, tkk)],
                                      jnp.concatenate(ps, axis=1), preferred_element_type=F32))
        finalize(acc_ref[...], jnp.concatenate(lx, axis=1))


def _diff_attention(q, kvs, lamv, sg_col, n_batch, sq, tq, tk, lam_init):
    m = q.shape[0]
    nq = sq // tq
    in_specs = [pl.BlockSpec((tq, HEAD_W), lambda b, h, i: (b * nq + i, h))]
    args = [q]
    src_lens = []
    for (k, vt) in kvs:
        sl = k.shape[0] // n_batch
        src_lens.append(sl)
        in_specs.append(pl.BlockSpec((sl, HEAD_W), lambda b, h, i: (b, h)))
        in_specs.append(pl.BlockSpec((HEAD_W, sl), lambda b, h, i: (b * N_HEADS + h, 0)))
        args += [k, vt]
    in_specs += [pl.BlockSpec(lamv.shape, lambda b, h, i: (0, 0)),
                 pl.BlockSpec((HEAD_W, 1), lambda b, h, i: (0, 0)),
                 pl.BlockSpec((HEAD_W, HEAD_W), lambda b, h, i: (0, 0))]
    gid = jnp.arange(HEAD_W) // HEAD_DIM
    args += [lamv, sg_col, (gid[:, None] == gid[None, :]).astype(BF16)]
    return pl.pallas_call(
        functools.partial(_attn_kernel, src_lens=tuple(src_lens), tq=tq, tk=tk,
                          lam_init=lam_init),
        grid=(n_batch, N_HEADS, nq),
        in_specs=in_specs,
        out_specs=pl.BlockSpec((tq, HEAD_W), lambda b, h, i: (b * nq + i, h)),
        out_shape=jax.ShapeDtypeStruct((m, N_HEADS * HEAD_W), BF16),
        scratch_shapes=[pltpu.VMEM((HEAD_W, 2 * tq), F32), pltpu.VMEM((1, HEAD_W), F32),
                        pltpu.VMEM((tk, 2 * tq), F32), pltpu.VMEM((tk, 2 * tq), F32)],
        compiler_params=_cparams(("parallel", "parallel", "arbitrary")),
        name="diff_attention",
    )(*args)


def _merge_kernel(x_ref, ao_ref, u_ref, up_ref, un_ref, gates_ref, mod_ref, dw_ref, dwb_ref,
                  lng_ref, lnb_ref, wco_ref, wao_ref, wout_ref, o_ref, ubuf, cbuf, *, tm, tps, rc):
    i = pl.program_id(0)
    first = (i % tps) == 0
    last = (i % tps) == tps - 1
    ubuf[0:CONV_HALO, :] = jnp.where(first, 0.0, up_ref[...])
    ubuf[CONV_HALO:CONV_HALO + tm, :] = u_ref[...]
    ubuf[CONV_HALO + tm:2 * CONV_HALO + tm, :] = jnp.where(last, 0.0, un_ref[...])
    shift = CONV_HALO - CONV_K // 2
    win = rc + 2 * CONV_HALO

    def chunk(r, carry):
        r0 = pl.multiple_of(r * rc, rc)
        for cb in range(D_MODEL // LANES):
            sl = slice(cb * LANES, (cb + 1) * LANES)
            w = ubuf[pl.ds(r0, win), sl]
            acc = jnp.zeros((rc, LANES), F32)
            for rot in range(8):
                wr = w if rot == 0 else pltpu.roll(w, win - rot, 0)
                for k in range(CONV_K):
                    if (k + shift) % 8 == rot:
                        a0 = (k + shift) - rot
                        acc = acc + wr[a0:a0 + rc, :] * dw_ref[k:k + 1, sl]
            cbuf[pl.ds(r0, rc), sl] = acc + dwb_ref[:, sl]
        return carry

    lax.fori_loop(0, tm // rc, chunk, 0)

    cv = cbuf[...]
    mu = jnp.mean(cv, axis=-1, keepdims=True)
    xc = cv - mu
    var = jnp.mean(xc * xc, axis=-1, keepdims=True)
    y = xc * lax.rsqrt(var + EPS) * lng_ref[...] + lnb_ref[...]
    y_conv = jnp.dot(_silu(y).astype(BF16), wco_ref[...], preferred_element_type=F32)
    y_attn = jnp.dot(ao_ref[...], wao_ref[...], preferred_element_type=F32)
    mix = (gates_ref[:, 0:D_MODEL].astype(F32) * y_attn
           + gates_ref[:, D_MODEL:2 * D_MODEL].astype(F32) * y_conv)
    out = jnp.dot(mix.astype(BF16), wout_ref[...], preferred_element_type=F32)
    o_ref[...] = x_ref[...] + mod_ref[:, 2 * D_MODEL:3 * D_MODEL] * out


def _merge(x, ao, u, gates, mod3, row_of, dw, dwb, lng, lnb, wco, wao, wout, seq, tm):
    m, d = x.shape
    tps = seq // tm
    hb = tm // CONV_HALO
    n_halo = m // CONV_HALO
    row = lambda i: (i, 0)
    const = lambda i: (0, 0)
    in_specs = [
        pl.BlockSpec((tm, d), row),
        pl.BlockSpec((tm, d), row),
        pl.BlockSpec((tm, d), row),
        pl.BlockSpec((CONV_HALO, d), lambda i: (jnp.maximum(i * hb - 1, 0), 0)),
        pl.BlockSpec((CONV_HALO, d), lambda i: (jnp.minimum((i + 1) * hb, n_halo - 1), 0)),
        pl.BlockSpec((tm, 2 * d), row),
        pl.BlockSpec((None, 1, mod3.shape[2]), lambda i: (row_of(i), 0, 0)),
        pl.BlockSpec(dw.shape, const),
        pl.BlockSpec((1, d), const), pl.BlockSpec((1, d), const), pl.BlockSpec((1, d), const),
        pl.BlockSpec((d, d), const), pl.BlockSpec((d, d), const), pl.BlockSpec((d, d), const),
    ]
    return pl.pallas_call(
        functools.partial(_merge_kernel, tm=tm, tps=tps, rc=64),
        grid=(m // tm,),
        in_specs=in_specs,
        out_specs=pl.BlockSpec((tm, d), row),
        out_shape=jax.ShapeDtypeStruct((m, d), F32),
        scratch_shapes=[pltpu.VMEM((tm + 2 * CONV_HALO, d), F32), pltpu.VMEM((tm, d), F32)],
        compiler_params=_cparams(("parallel",)),
        name="conv_merge",
    )(x, ao, u, u, u, gates, mod3, dw, dwb, lng, lnb, wco, wao, wout)


def _swiglu_tile(hb, wg_ref, wu_ref, wd_ref, n_split):
    dff = wg_ref.shape[-1]
    step = dff // n_split
    y = None
    for f0 in range(0, dff, step):
        a = jnp.dot(hb, wg_ref[:, f0:f0 + step], preferred_element_type=F32)
        b = jnp.dot(hb, wu_ref[:, f0:f0 + step], preferred_element_type=F32)
        part = jnp.dot((_silu(a) * b).astype(BF16), wd_ref[f0:f0 + step, :],
                       preferred_element_type=F32)
        y = part if y is None else y + part
    return y


def _ffn_kernel(x_ref, mod_ref, g_ref, wg_ref, wu_ref, wd_ref, o_ref, *, n_split):
    h = _norm_modulate(x_ref[...], g_ref[...], mod_ref[:, 3 * D_MODEL:4 * D_MODEL],
                       mod_ref[:, 4 * D_MODEL:5 * D_MODEL]).astype(BF16)
    y = _swiglu_tile(h, wg_ref, wu_ref, wd_ref, n_split)
    o_ref[...] = x_ref[...] + mod_ref[:, 5 * D_MODEL:6 * D_MODEL] * y


def _dense_ffn(x, mod3, row_of, g, wg, wu, wd, tm, tf):
    m, d = x.shape
    dff = wg.shape[1]
    resident = pl.Buffered(1)
    return pl.pallas_call(
        functools.partial(_ffn_kernel, n_split=dff // tf),
        grid=(m // tm,),
        in_specs=[pl.BlockSpec((tm, d), lambda i: (i, 0)),
                  pl.BlockSpec((None, 1, mod3.shape[2]), lambda i: (row_of(i), 0, 0)),
                  pl.BlockSpec((1, d), lambda i: (0, 0)),
                  pl.BlockSpec((d, dff), lambda i: (0, 0), pipeline_mode=resident),
                  pl.BlockSpec((d, dff), lambda i: (0, 0), pipeline_mode=resident),
                  pl.BlockSpec((dff, d), lambda i: (0, 0), pipeline_mode=resident)],
        out_specs=pl.BlockSpec((tm, d), lambda i: (i, 0)),
        out_shape=jax.ShapeDtypeStruct((m, d), F32),
        compiler_params=_cparams(("parallel",)),
        name="dense_ffn",
    )(x, mod3, g, wg, wu, wd)


R_E0, R_E1, R_RANK0, R_RANK1, R_W0, R_W1 = range(6)
ITEM_VALID, ITEM_FIRST, ITEM_LAST, ITEM_ZERO = 1, 2, 4, 8


def _router_kernel(x_ref, mod_ref, g_ref, wr_ref, h_ref, route_ref, cum_ref, base_ref):
    @pl.when(pl.program_id(0) == 0)
    def _():
        base_ref[...] = jnp.zeros(base_ref.shape, F32)

    h = _norm_modulate(x_ref[...], g_ref[...], mod_ref[:, 3 * D_MODEL:4 * D_MODEL],
                       mod_ref[:, 4 * D_MODEL:5 * D_MODEL])
    h_ref[...] = h.astype(BF16)
    logits = jnp.dot(h, wr_ref[...], preferred_element_type=F32, precision=lax.Precision.HIGHEST)
    tb = logits.shape[0]
    lane = lax.broadcasted_iota(jnp.int32, logits.shape, 1)
    lg = jnp.where(lane < N_EXPERTS, logits, NEG_BIG)
    m1 = jnp.max(lg, axis=1, keepdims=True)
    e0 = jnp.min(jnp.where(lg == m1, lane, LANES), axis=1, keepdims=True)
    lg2 = jnp.where(lane == e0, NEG_BIG, lg)
    m2 = jnp.max(lg2, axis=1, keepdims=True)
    e1 = jnp.min(jnp.where(lg2 == m2, lane, LANES), axis=1, keepdims=True)
    ex = jnp.exp(m2 - m1)
    w0 = 1.0 / (1.0 + ex)
    w1 = ex / (1.0 + ex)
    oh0 = lane == e0
    oh1 = lane == e1
    oh = jnp.where(oh0, 1.0, 0.0) + jnp.where(oh1, 1.0, 0.0)
    ltri = jnp.where(lax.broadcasted_iota(jnp.int32, (tb, tb), 1)
                     < lax.broadcasted_iota(jnp.int32, (tb, tb), 0), 1.0, 0.0).astype(BF16)
    rank = jnp.dot(ltri, oh.astype(BF16), preferred_element_type=F32) + base_ref[...]
    rank0 = jnp.sum(jnp.where(oh0, rank, 0.0), axis=1, keepdims=True)
    rank1 = jnp.sum(jnp.where(oh1, rank, 0.0), axis=1, keepdims=True)
    route = jnp.zeros(logits.shape, F32)
    for j, val in ((R_E0, e0.astype(F32)), (R_E1, e1.astype(F32)), (R_RANK0, rank0),
                   (R_RANK1, rank1), (R_W0, w0), (R_W1, w1)):
        route = jnp.where(lane == j, val, route)
    route_ref[...] = route
    base = base_ref[...] + jnp.sum(oh, axis=0, keepdims=True)
    base_ref[...] = base
    cum_ref[...] = base


def _router(x, mod3, row_of, g, wr_pad, tb):
    m, d = x.shape
    nb = m // tb
    return pl.pallas_call(
        _router_kernel,
        grid=(nb,),
        in_specs=[pl.BlockSpec((tb, d), lambda i: (i, 0)),
                  pl.BlockSpec((None, 1, mod3.shape[2]), lambda i: (row_of(i), 0, 0)),
                  pl.BlockSpec((1, d), lambda i: (0, 0)),
                  pl.BlockSpec((d, LANES), lambda i: (0, 0))],
        out_specs=[pl.BlockSpec((tb, d), lambda i: (i, 0)),
                   pl.BlockSpec((tb, LANES), lambda i: (i, 0)),
                   pl.BlockSpec((None, 1, LANES), lambda i: (i, 0, 0))],
        out_shape=[jax.ShapeDtypeStruct((m, d), BF16),
                   jax.ShapeDtypeStruct((m, LANES), F32),
                   jax.ShapeDtypeStruct((nb, 1, LANES), F32)],
        scratch_shapes=[pltpu.VMEM((1, LANES), F32)],
        compiler_params=_cparams(("arbitrary",)),
        name="moe_router",
    )(x, mod3, g, wr_pad)


def _moe_plan(route, cum, nb, tb, tw):
    i32 = jnp.int32
    wpt = tb // tw
    span = wpt + 1
    n_tiles = 2 * nb + N_EXPERTS
    n_items = n_tiles * wpt + N_EXPERTS * nb
    cum_in = cum[:, 0, :N_EXPERTS].astype(i32)
    cum_ex = jnp.concatenate([jnp.zeros((1, N_EXPERTS), i32), cum_in[:-1]], axis=0)
    counts = cum_in[-1]
    ntile = (counts + tb - 1) // tb
    tile_end = jnp.cumsum(ntile)
    start = (tile_end - ntile) * tb
    n_used = tile_end[-1]
    t_idx = jnp.arange(n_tiles, dtype=i32)
    tile_valid = (t_idx < n_used).astype(i32)
    tile_src = jnp.minimum(t_idx, n_used - 1)
    tile_expert = jnp.minimum(jnp.sum((tile_src[:, None] >= tile_end[None, :]).astype(i32), axis=1),
                              N_EXPERTS - 1)
    e0 = route[:, R_E0].astype(i32)
    e1 = route[:, R_E1].astype(i32)
    pos0 = start[e0] + route[:, R_RANK0].astype(i32)
    pos1 = start[e1] + route[:, R_RANK1].astype(i32)
    lo = start[None, :] + cum_ex
    hi = start[None, :] + cum_in
    w_lo = lo // tw
    npair = jnp.where(hi > lo, (hi - 1) // tw - w_lo + 1, 0)
    j = jnp.arange(span, dtype=i32)
    c_valid = j[None, None, :] < npair[:, :, None]
    c_win = w_lo[:, :, None] + j[None, None, :]
    c_blk = jnp.broadcast_to(jnp.arange(nb, dtype=i32)[:, None, None], c_valid.shape)
    jp = jnp.arange(wpt - 1, dtype=i32)
    p_win = ((start + counts + tw - 1) // tw)[:, None] + jp[None, :]
    p_valid = p_win < (tile_end * wpt)[:, None]

    def compact(valid, cols, n):
        order = jnp.argsort(jnp.logical_not(valid), stable=True)[:n]
        n_valid = jnp.sum(valid.astype(i32))
        k = jnp.minimum(jnp.arange(n, dtype=i32), n_valid - 1)
        return [c[order][k] for c in cols], jnp.arange(n, dtype=i32) < n_valid

    def flags(valid, key, extra):
        prev = jnp.concatenate([key[:1] - 1, key[:-1]])
        nxt = jnp.concatenate([key[1:], key[-1:] - 1])
        last_valid = jnp.concatenate([jnp.logical_not(valid[1:]), jnp.ones((1,), bool)])
        return (valid.astype(i32) * ITEM_VALID
                + (valid & (key != prev)).astype(i32) * ITEM_FIRST
                + (valid & ((key != nxt) | last_valid)).astype(i32) * ITEM_LAST
                + (valid & extra).astype(i32) * ITEM_ZERO)

    em = lambda a: jnp.transpose(a, (1, 0, 2)).reshape(-1)
    bm = lambda a: a.reshape(-1)
    n_pad = N_EXPERTS * (wpt - 1)
    g_valid = jnp.concatenate([em(c_valid), p_valid.reshape(-1)])
    g_win = jnp.concatenate([em(c_win), p_win.reshape(-1)])
    g_blk = jnp.concatenate([em(c_blk), jnp.zeros((n_pad,), i32)])
    g_zero = jnp.concatenate([jnp.zeros(em(c_valid).shape, bool), jnp.ones((n_pad,), bool)])
    (g_win, g_blk, g_zero), g_ok = compact(g_valid, [g_win, g_blk, g_zero], n_items + n_pad)
    (c_win2, c_blk2), c_ok = compact(bm(c_valid), [bm(c_win), bm(c_blk)], n_items)
    return dict(pos0=pos0, pos1=pos1, start=start, tile_src=tile_src, tile_expert=tile_expert,
                tile_valid=tile_valid, n_tiles=n_tiles,
                gather=(g_win, g_blk, flags(g_ok, g_win, g_zero)),
                combine=(c_blk2, c_win2, flags(c_ok, c_blk2, jnp.zeros_like(c_ok))))


def _gather_kernel(it_win, it_blk, it_flag, p0_ref, p1_ref, h_ref, o_ref, *, tb, tw):
    k = pl.program_id(0)
    flag = it_flag[k]

    @pl.when((flag & ITEM_ZERO) != 0)
    def _():
        o_ref[...] = jnp.zeros(o_ref.shape, BF16)

    @pl.when(((flag & ITEM_VALID) != 0) & ((flag & ITEM_ZERO) == 0))
    def _():
        rows = lax.broadcasted_iota(jnp.int32, (tw, tb), 0) + it_win[k] * tw
        sel = jnp.where(p0_ref[...] == rows, 1.0, 0.0) + jnp.where(p1_ref[...] == rows, 1.0, 0.0)
        c = jnp.dot(sel.astype(BF16), h_ref[...], preferred_element_type=F32)

        @pl.when((flag & ITEM_FIRST) != 0)
        def _():
            o_ref[...] = c.astype(BF16)

        @pl.when((flag & ITEM_FIRST) == 0)
        def _():
            o_ref[...] = (o_ref[...].astype(F32) + c).astype(BF16)


def _moe_gather(h, plan, tb, tw):
    m, d = h.shape
    nb = m // tb
    it_win, it_blk, it_flag = plan["gather"]
    p0 = plan["pos0"].reshape(nb, 1, tb)
    p1 = plan["pos1"].reshape(nb, 1, tb)
    grid_spec = pltpu.PrefetchScalarGridSpec(
        num_scalar_prefetch=3,
        grid=(it_win.shape[0],),
        in_specs=[pl.BlockSpec((None, 1, tb), lambda k, w, b, f: (b[k], 0, 0)),
                  pl.BlockSpec((None, 1, tb), lambda k, w, b, f: (b[k], 0, 0)),
                  pl.BlockSpec((tb, d), lambda k, w, b, f: (b[k], 0))],
        out_specs=pl.BlockSpec((tw, d), lambda k, w, b, f: (w[k], 0)),
    )
    return pl.pallas_call(
        functools.partial(_gather_kernel, tb=tb, tw=tw),
        grid_spec=grid_spec,
        out_shape=jax.ShapeDtypeStruct((plan["n_tiles"] * tb, d), BF16),
        compiler_params=_cparams(("arbitrary",)),
        name="moe_gather",
    )(it_win, it_blk, it_flag, p0, p1, h)


def _expert_kernel(ts_ref, te_ref, tv_ref, x_ref, wg_ref, wu_ref, wd_ref, o_ref, *, n_split):
    @pl.when(tv_ref[pl.program_id(0)] != 0)
    def _():
        o_ref[...] = _swiglu_tile(x_ref[...], wg_ref, wu_ref, wd_ref, n_split).astype(BF16)


def _moe_experts(xs, plan, wg, wu, wd, tb, tf):
    d = xs.shape[1]
    dff = wg.shape[2]
    single = pl.Buffered(1)
    grid_spec = pltpu.PrefetchScalarGridSpec(
        num_scalar_prefetch=3,
        grid=(plan["n_tiles"],),
        in_specs=[pl.BlockSpec((tb, d), lambda t, ts, te, tv: (ts[t], 0)),
                  pl.BlockSpec((None, d, dff), lambda t, ts, te, tv: (te[t], 0, 0),
                               pipeline_mode=single),
                  pl.BlockSpec((None, d, dff), lambda t, ts, te, tv: (te[t], 0, 0),
                               pipeline_mode=single),
                  pl.BlockSpec((None, dff, d), lambda t, ts, te, tv: (te[t], 0, 0),
                               pipeline_mode=single)],
        out_specs=pl.BlockSpec((tb, d), lambda t, ts, te, tv: (ts[t], 0)),
    )
    return pl.pallas_call(
        functools.partial(_expert_kernel, n_split=dff // tf),
        grid_spec=grid_spec,
        out_shape=jax.ShapeDtypeStruct(xs.shape, BF16),
        compiler_params=_cparams(("arbitrary",)),
        name="moe_experts",
    )(plan["tile_src"], plan["tile_expert"], plan["tile_valid"], xs, wg, wu, wd)


def _combine_kernel(it_blk, it_win, it_flag, x_ref, mod_ref, pw_ref, y_ref, o_ref, acc_ref,
                    *, tb, tw):
    k = pl.program_id(0)
    flag = it_flag[k]

    @pl.when((flag & ITEM_VALID) != 0)
    def _():
        reps = tw // LANES
        lane_rep = lambda j: jnp.tile(pw_ref[:, j * LANES:(j + 1) * LANES], (1, reps))
        col = (lax.broadcasted_iota(jnp.int32, (tb, tw), 1) + it_win[k] * tw).astype(F32)
        sel = (jnp.where(col == lane_rep(0), lane_rep(2), 0.0)
               + jnp.where(col == lane_rep(1), lane_rep(3), 0.0))
        c = jnp.dot(sel.astype(BF16), y_ref[...], preferred_element_type=F32)

        @pl.when((flag & ITEM_FIRST) != 0)
        def _():
            acc_ref[...] = c

        @pl.when((flag & ITEM_FIRST) == 0)
        def _():
            acc_ref[...] += c

        @pl.when((flag & ITEM_LAST) != 0)
        def _():
            o_ref[...] = x_ref[...] + mod_ref[:, 5 * D_MODEL:6 * D_MODEL] * acc_ref[...]


def _moe_combine(x, mod3, row_of, route, ys, plan, tb, tw):
    m, d = x.shape
    it_blk, it_win, it_flag = plan["combine"]
    rep = lambda v: jnp.broadcast_to(v.astype(F32)[:, None], (m, LANES))
    pw = jnp.concatenate([rep(plan["pos0"]), rep(plan["pos1"]),
                          rep(route[:, R_W0]), rep(route[:, R_W1])], axis=1)
    grid_spec = pltpu.PrefetchScalarGridSpec(
        num_scalar_prefetch=3,
        grid=(it_blk.shape[0],),
        in_specs=[pl.BlockSpec((tb, d), lambda k, b, w, f: (b[k], 0)),
                  pl.BlockSpec((None, 1, mod3.shape[2]), lambda k, b, w, f: (row_of(b[k]), 0, 0)),
                  pl.BlockSpec((tb, 4 * LANES), lambda k, b, w, f: (b[k], 0)),
                  pl.BlockSpec((tw, d), lambda k, b, w, f: (w[k], 0))],
        out_specs=pl.BlockSpec((tb, d), lambda k, b, w, f: (b[k], 0)),
        scratch_shapes=[pltpu.VMEM((tb, d), F32)],
    )
    return pl.pallas_call(
        functools.partial(_combine_kernel, tb=tb, tw=tw),
        grid_spec=grid_spec,
        out_shape=jax.ShapeDtypeStruct((m, d), F32),
        compiler_params=_cparams(("arbitrary",)),
        name="moe_combine",
    )(it_blk, it_win, it_flag, x, mod3, pw, ys)


def _moe_ffn(x, mod3, row_of, g, wr_pad, wg, wu, wd, tb, tf):
    nb = x.shape[0] // tb
    tw = min(MOE_WINDOW, tb)
    h, route, cum = _router(x, mod3, row_of, g, wr_pad, tb)
    plan = _moe_plan(route, cum, nb, tb, tw)
    xs = _moe_gather(h, plan, tb, tw)
    ys = _moe_experts(xs, plan, wg, wu, wd, tb, tf)
    return _moe_combine(x, mod3, row_of, route, ys, plan, tb, tw)


def _rope_tables(seq):
    t = jnp.arange(seq, dtype=jnp.int32)
    row = (t // GRID_W).astype(F32)
    col = (t % GRID_W).astype(F32)
    inv = ROPE_THETA ** (-jnp.arange(0, ROT_AXIS, 2, dtype=F32) / ROT_AXIS)
    ar = row[:, None] * inv
    ac = col[:, None] * inv
    ang = jnp.concatenate([ar, ar, ac, ac], axis=-1)
    ang = jnp.concatenate([ang, ang], axis=-1)
    cos, sin = jnp.cos(ang), jnp.sin(ang)
    upper = (jnp.arange(HEAD_W) & ROT_HALF) != 0
    sa = jnp.where(upper, 0.0, -sin)
    sb = jnp.where(upper, sin, 0.0)
    return cos, sa, sb


def _pick_tile(n, pref):
    t = min(n, pref)
    while n % t:
        t //= 2
    return t


def kernel(x, c, ctx, c_ctx, w_mod, b_mod, g_mix, w_in, q_norm_g, k_norm_g, lambda_q1, lambda_k1, lambda_q2, lambda_k2, subln_g, w_attn_o, dw_weight, dw_bias, conv_ln_g, conv_ln_b, w_conv_o, w_out, g_ffn, w_ff_gate, w_ff_up, w_ff_down, w_router, w_exp_gate, w_exp_up, w_exp_down):
    n_batch, seq, d = x.shape
    ctx_len = ctx.shape[1]
    depth = w_mod.shape[0]
    assert d == D_MODEL and seq % GRID_W == 0

    tm_l = _pick_tile(seq, 512)
    tm_c = _pick_tile(ctx_len, 512)
    ctx_row = n_batch
    mod_rows = -(-(n_batch + 1) // 8) * 8
    cvec = jnp.zeros((mod_rows, d), F32).at[:n_batch].set(c).at[ctx_row].set(c_ctx)

    cos, sa, sb = _rope_tables(seq)
    gid = jnp.arange(HEAD_W) // HEAD_DIM
    gsum = ((gid[:, None] == gid[None, :]).astype(F32) / HEAD_DIM).astype(BF16)

    def rope_with_gain(g_row):
        return (cos * g_row, sa * jnp.roll(g_row, -ROT_HALF, axis=1),
                sb * jnp.roll(g_row, ROT_HALF, axis=1))

    xl = x.reshape(n_batch * seq, d)
    xc = ctx.reshape(n_batch * ctx_len, d)
    lat_row = lambda i: (i * tm_l) // seq
    ctx_row_of = lambda i: ctx_row

    for l in range(depth):
        last = l == depth - 1
        lam_init = 0.8 - 0.6 * math.exp(-0.3 * l)
        mod3 = _modulation(cvec, w_mod[l], b_mod[l]).reshape(mod_rows, 1, 6 * d)
        w_bf = w_in[l].astype(BF16)
        gmix = g_mix[l].reshape(1, d)
        qg = jnp.tile(q_norm_g[l], 2).reshape(1, HEAD_W) * Q_SCALE
        kg = jnp.tile(k_norm_g[l], 2).reshape(1, HEAD_W)
        tabs = rope_with_gain(qg) + rope_with_gain(kg)
        lamv = jnp.stack([lambda_q1[l], lambda_k1[l], lambda_q2[l], lambda_k2[l]])
        sg = subln_g[l].reshape(HEAD_W, 1)
        dw = jnp.zeros((32, d), F32).at[:CONV_K].set(dw_weight[l])
        merge_w = (dw, dw_bias[l].reshape(1, d), conv_ln_g[l].reshape(1, d),
                   conv_ln_b[l].reshape(1, d), w_conv_o[l].astype(BF16),
                   w_attn_o[l].astype(BF16), w_out[l].astype(BF16))

        q, k, v, u, gates = _in_proj_full(xl, mod3, lat_row, gmix, w_bf, qg, kg, gsum, tabs,
                                          seq, tm_l)
        if last:
            _, kc, vc = _proj(xc, mod3, ctx_row_of, gmix, w_bf, qg, kg, gsum, None,
                              ctx_len, GROUPS_KV, True, tm_c)
        else:
            qc, kc, vc, uc, gatesc = _in_proj_full(xc, mod3, ctx_row_of, gmix, w_bf, qg, kg,
                                                   gsum, None, ctx_len, tm_c)
        tk = ATTN_TK
        ao = _diff_attention(q, [(k, v), (kc, vc)], lamv, sg, n_batch, seq,
                             _pick_tile(seq, ATTN_TQ), tk, lam_init)
        xl = _merge(xl, ao, u, gates, mod3, lat_row, *merge_w, seq, tm_l)
        if not last:
            aoc = _diff_attention(qc, [(kc, vc)], lamv, sg, n_batch, ctx_len,
                                  _pick_tile(ctx_len, ATTN_TQ), tk, lam_init)
            xc = _merge(xc, aoc, uc, gatesc, mod3, ctx_row_of, *merge_w, ctx_len, tm_c)

        gf = g_ffn[l].reshape(1, d)
        i = l // 2
        if l % 2 == 0:
            wts = (w_ff_gate[i].astype(BF16), w_ff_up[i].astype(BF16), w_ff_down[i].astype(BF16))
            tf = _pick_ff_tile(wts[0].shape[1])
            xl = _dense_ffn(xl, mod3, lat_row, gf, *wts, tm_l, tf)
            if not last:
                xc = _dense_ffn(xc, mod3, ctx_row_of, gf, *wts, tm_c, tf)
        else:
            wr_pad = jnp.zeros((d, LANES), F32).at[:, :N_EXPERTS].set(w_router[i])
            wts = (w_exp_gate[i].astype(BF16), w_exp_up[i].astype(BF16),
                   w_exp_down[i].astype(BF16))
            tf = _pick_ff_tile(wts[0].shape[2])
            xl = _moe_ffn(xl, mod3, lat_row, gf, wr_pad, *wts, tm_l, tf)
            if not last:
                xc = _moe_ffn(xc, mod3, ctx_row_of, gf, wr_pad, *wts, tm_c, tf)
    return xl.reshape(n_batch, seq, d)


def _pick_ff_tile(dff):
    best = LANES
    for t in range(LANES, 1792 + 1, LANES):
        if dff % t == 0:
            best = t
    return best
```

```python
import functools
import math

import jax
import jax.numpy as jnp
from jax import lax
from jax.experimental import pallas as pl
from jax.experimental.pallas import tpu as pltpu

F32 = jnp.float32
BF16 = jnp.bfloat16

LANES = 128
D_MODEL = 1024
N_HEADS = 8
HEAD_DIM = 64
HEAD_W = 2 * HEAD_DIM
GRID_W = 64
ROT_AXIS = HEAD_DIM // 2
ROT_HALF = ROT_AXIS // 2
ROPE_THETA = 10000.0
CONV_K = 31
CONV_HALO = 16
N_EXPERTS = 8
EPS = 1e-6
N_GROUPS = 7
ATTN_TQ = 1024
ATTN_TK = 1024
MOE_WINDOW = 256
KNORM_ROWS = 512
BOUND_SLACK = 1.01
MIN_DENOM = 2.0 ** -60
VMEM_LIMIT = 52 * 1024 * 1024
Q_SCALE = (HEAD_DIM ** -0.5) * math.log2(math.e)
NEG_BIG = -1e30


def _cparams(sem):
    return pltpu.CompilerParams(dimension_semantics=sem, vmem_limit_bytes=VMEM_LIMIT)


def _sigmoid(v):
    return 0.5 * jnp.tanh(0.5 * v) + 0.5


def _silu(v):
    return v * _sigmoid(v)


def _norm_modulate(xf, g, shift, scale):
    ms = jnp.mean(xf * xf, axis=-1, keepdims=True)
    return (xf * lax.rsqrt(ms + EPS) * g) * (1.0 + scale) + shift


def _mod_kernel(c_ref, w_ref, b_ref, o_ref):
    a = _silu(c_ref[...])
    o_ref[...] = jnp.dot(a, w_ref[...], preferred_element_type=F32,
                         precision=lax.Precision.HIGHEST) + b_ref[...]


def _modulation(cvec, w_all, b_all, layer):
    rows, d = cvec.shape
    n = w_all.shape[2]
    tn = n // 4
    return pl.pallas_call(
        _mod_kernel,
        grid=(n // tn,),
        in_specs=[pl.BlockSpec((rows, d), lambda j: (0, 0)),
                  pl.BlockSpec((None, d, tn), lambda j: (layer, 0, j)),
                  pl.BlockSpec((None, 1, tn), lambda j: (layer, 0, j))],
        out_specs=pl.BlockSpec((rows, tn), lambda j: (0, j)),
        out_shape=jax.ShapeDtypeStruct((rows, n), F32),
        compiler_params=_cparams(("arbitrary",)),
        name="modulation",
    )(cvec, w_all, b_all.reshape(b_all.shape[0], 1, n))


G_Q, G_K, G_V, G_GLU_A, G_GLU_G, G_GATE_A, G_GATE_C = range(N_GROUPS)
GROUPS_QK = (G_Q, G_K)
GROUPS_REST = (G_V, G_GLU_A, G_GLU_G, G_GATE_A, G_GATE_C)
GROUPS_KV = (G_K, G_V)


def _proj_kernel(*refs, groups, from_x, rope):
    if from_x:
        x_ref, mod_ref, gmix_ref = refs[:3]
        pos = 3
    else:
        hin_ref = refs[0]
        pos = 1
    w_refs = refs[pos:pos + len(groups)]
    pos += len(groups)
    qg_ref, kg_ref, gsum_ref = refs[pos:pos + 3]
    pos += 3
    tabs = {}
    if rope:
        for g in (G_Q, G_K):
            if g in groups:
                tabs[g] = refs[pos:pos + 3]
                pos += 3
    outs = list(refs[pos:])
    if from_x:
        h = _norm_modulate(x_ref[...], gmix_ref[...], mod_ref[:, 0:D_MODEL],
                           mod_ref[:, D_MODEL:2 * D_MODEL]).astype(BF16)
        h_out = outs.pop(0)
        h_out[...] = h
    else:
        h = hin_ref[...]
    out_of = {}
    for g in groups:
        if g in (G_Q, G_K, G_V):
            out_of[g] = outs.pop(0)
    if G_GLU_A in groups:
        u_ref = outs.pop(0)
    if G_GATE_A in groups:
        gates_ref = outs.pop(0)

    def qk_norm(acc, out_ref, g_row_ref, tables):
        gsum = gsum_ref[...]
        for hb in range(N_HEADS):
            sl = slice(hb * HEAD_W, (hb + 1) * HEAD_W)
            blk = acc[:, sl]
            ms = jnp.dot((blk * blk).astype(BF16), gsum, preferred_element_type=F32)
            y = blk * lax.rsqrt(ms + EPS)
            if tables is not None:
                c_ref, sa_ref, sb_ref = tables
                y = (y * c_ref[...] + pltpu.roll(y, HEAD_W - ROT_HALF, 1) * sa_ref[...]
                     + pltpu.roll(y, ROT_HALF, 1) * sb_ref[...])
            else:
                y = y * g_row_ref[...]
            out_ref[:, sl] = y.astype(BF16)

    glu_a = None
    for g, w_ref in zip(groups, w_refs):
        acc = jnp.dot(h, w_ref[...], preferred_element_type=F32)
        if g == G_Q:
            qk_norm(acc, out_of[g], qg_ref, tabs.get(g))
        elif g == G_K:
            qk_norm(acc, out_of[g], kg_ref, tabs.get(g))
        elif g == G_V:
            out_of[g][...] = acc.T.astype(BF16)
        elif g == G_GLU_A:
            glu_a = acc
        elif g == G_GLU_G:
            u_ref[...] = glu_a * _sigmoid(acc)
        elif g == G_GATE_A:
            gates_ref[:, 0:D_MODEL] = _sigmoid(acc).astype(BF16)
        elif g == G_GATE_C:
            gates_ref[:, D_MODEL:2 * D_MODEL] = _sigmoid(acc).astype(BF16)


def _proj(src, mod3, row_of, gmix, w_bf, qg, kg, gsum, rope_tabs, seq, groups, from_x, tm):
    m, d = src.shape
    tps = seq // tm
    rope = rope_tabs is not None
    const = lambda i: (0, 0)
    row = lambda i: (i, 0)
    if from_x:
        in_specs = [pl.BlockSpec((tm, d), row),
                    pl.BlockSpec((None, 1, mod3.shape[2]), lambda i: (row_of(i), 0, 0)),
                    pl.BlockSpec((1, d), const)]
        args = [src, mod3, gmix]
    else:
        in_specs = [pl.BlockSpec((tm, d), row)]
        args = [src]
    for g in groups:
        in_specs.append(pl.BlockSpec((d, D_MODEL), lambda i, g=g: (0, g)))
        args.append(w_bf)
    in_specs += [pl.BlockSpec((1, HEAD_W), const), pl.BlockSpec((1, HEAD_W), const),
                 pl.BlockSpec((HEAD_W, HEAD_W), const)]
    args += [qg, kg, gsum]
    if rope:
        for g, t3 in ((G_Q, rope_tabs[0:3]), (G_K, rope_tabs[3:6])):
            if g in groups:
                for t in t3:
                    in_specs.append(pl.BlockSpec((tm, HEAD_W), lambda i: (i % tps, 0)))
                    args.append(t)
    row_spec = pl.BlockSpec((tm, D_MODEL), row)
    rows_bf = jax.ShapeDtypeStruct((m, D_MODEL), BF16)
    out_specs, out_shape = [], []
    if from_x:
        out_specs.append(row_spec)
        out_shape.append(rows_bf)
    for g in groups:
        if g in (G_Q, G_K):
            out_specs.append(row_spec)
            out_shape.append(rows_bf)
        elif g == G_V:
            out_specs.append(pl.BlockSpec((D_MODEL, tm), lambda i: (i // tps, i % tps)))
            out_shape.append(jax.ShapeDtypeStruct(((m // seq) * D_MODEL, seq), BF16))
    if G_GLU_A in groups:
        out_specs.append(row_spec)
        out_shape.append(jax.ShapeDtypeStruct((m, D_MODEL), F32))
    if G_GATE_A in groups:
        out_specs.append(pl.BlockSpec((tm, 2 * D_MODEL), row))
        out_shape.append(jax.ShapeDtypeStruct((m, 2 * D_MODEL), BF16))
    return pl.pallas_call(
        functools.partial(_proj_kernel, groups=groups, from_x=from_x, rope=rope),
        grid=(m // tm,),
        in_specs=in_specs,
        out_specs=out_specs,
        out_shape=out_shape,
        compiler_params=_cparams(("parallel",)),
        name="in_proj",
    )(*args)


def _in_proj_full(x, mod3, row_of, gmix, w_bf, qg, kg, gsum, rope_tabs, seq, tm):
    h, q, k = _proj(x, mod3, row_of, gmix, w_bf, qg, kg, gsum, rope_tabs, seq, GROUPS_QK, True, tm)
    vt, u, gates = _proj(h, mod3, row_of, gmix, w_bf, qg, kg, gsum, None, seq, GROUPS_REST,
                         False, tm)
    return q, k, vt, u, gates


def _attn_kernel(*refs, src_lens, tq, tk, lam_init):
    n_src = len(src_lens)
    q_ref = refs[0]
    kv_refs = refs[1:1 + 2 * n_src]
    lamv_ref, sg_ref, ksum_ref, o_ref, acc_ref, kmax_ref = refs[1 + 2 * n_src:7 + 2 * n_src]
    s_refs = refs[7 + 2 * n_src:9 + 2 * n_src]
    n_cb = 2 * tq // LANES

    @pl.when(pl.program_id(2) == 0)
    def _():
        best = jnp.zeros((1, HEAD_W), F32)
        for si in range(n_src):
            for off in range(0, src_lens[si], KNORM_ROWS):
                rows = min(KNORM_ROWS, src_lens[si] - off)
                kf = kv_refs[2 * si][off:off + rows, :].astype(F32)
                n2 = jnp.dot((kf * kf).astype(BF16), ksum_ref[...], preferred_element_type=F32)
                best = jnp.maximum(best, jnp.max(n2, axis=0, keepdims=True))
        kmax_ref[...] = jnp.sqrt(best) * BOUND_SLACK

    q = q_ref[...]
    lane = lax.broadcasted_iota(jnp.int32, q.shape, 1)
    zero = jnp.zeros_like(q)
    q2 = jnp.concatenate([jnp.where(lane < HEAD_DIM, q, zero),
                          jnp.where(lane >= HEAD_DIM, q, zero)], axis=0)
    chunks = [(si, off, min(tk, src_lens[si])) for si in range(n_src)
              for off in range(0, src_lens[si], min(tk, src_lens[si]))]

    def scores(si, off, tkk):
        kc = kv_refs[2 * si][pl.ds(off, tkk), :]
        return lax.dot_general(kc, q2, (((1,), (1,)), ((), ())), preferred_element_type=F32)

    def finalize(acc, l):
        lv = lamv_ref[...]
        lam = (jnp.exp(jnp.sum(lv[0:1] * lv[1:2], axis=1, keepdims=True))
               - jnp.exp(jnp.sum(lv[2:3] * lv[3:4], axis=1, keepdims=True)) + lam_init)
        inv = 1.0 / l
        o_t = acc[:, 0:tq] * inv[:, 0:tq] - lam * (acc[:, tq:2 * tq] * inv[:, tq:2 * tq])
        ms = jnp.mean(o_t * o_t, axis=0, keepdims=True)
        y_t = o_t * lax.rsqrt(ms + EPS) * (sg_ref[...] * (1.0 - lam_init))
        o_ref[...] = y_t.T.astype(BF16)

    q2f = q2.astype(F32)
    qn2 = lax.dot_general(jnp.ones((8, HEAD_W), BF16), (q2f * q2f).astype(BF16),
                          (((1,), (1,)), ((), ())), preferred_element_type=F32)[0:1, :]
    kmax = kmax_ref[...]
    kmax_row = jnp.concatenate([jnp.broadcast_to(kmax[:, 0:1], (1, tq)),
                                jnp.broadcast_to(kmax[:, HEAD_DIM:HEAD_DIM + 1], (1, tq))], axis=1)
    bound = jnp.sqrt(qn2) * BOUND_SLACK * kmax_row
    s_refs[0][0:chunks[0][2], :] = scores(*chunks[0])
    l = jnp.zeros((1, 2 * tq), F32)
    for ci, (si, off, tkk) in enumerate(chunks):
        if ci + 1 < len(chunks):
            s_refs[(ci + 1) % 2][0:chunks[ci + 1][2], :] = scores(*chunks[ci + 1])
        p = jnp.exp2(s_refs[ci % 2][0:tkk, :] - bound)
        l = l + jnp.sum(p, axis=0, keepdims=True)
        pv = jnp.dot(kv_refs[2 * si + 1][:, pl.ds(off, tkk)], p.astype(BF16),
                     preferred_element_type=F32)
        if ci == 0:
            acc_ref[...] = pv
        else:
            acc_ref[...] += pv
    healthy = jnp.min(l) >= MIN_DENOM

    @pl.when(healthy)
    def _():
        finalize(acc_ref[...], l)

    @pl.when(jnp.logical_not(healthy))
    def _():
        acc_ref[...] = jnp.zeros(acc_ref.shape, F32)
        m = [jnp.full((1, LANES), NEG_BIG, F32) for _ in range(n_cb)]
        lx = [jnp.zeros((1, LANES), F32) for _ in range(n_cb)]
        for si, off, tkk in chunks:
            s = scores(si, off, tkk)
            alphas, ps = [], []
            for cb in range(n_cb):
                s_cb = s[:, cb * LANES:(cb + 1) * LANES]
                m_new = jnp.maximum(m[cb], jnp.max(s_cb, axis=0, keepdims=True))
                alpha = jnp.exp2(m[cb] - m_new)
                p_cb = jnp.exp2(s_cb - m_new)
                lx[cb] = alpha * lx[cb] + jnp.sum(p_cb, axis=0, keepdims=True)
                alphas.append(alpha)
                ps.append(p_cb.astype(BF16))
                m[cb] = m_new
            acc_ref[...] = (jnp.concatenate(alphas, axis=1) * acc_ref[...]
                            + jnp.dot(kv_refs[2 * si + 1][:, pl.ds(off, tkk)],
                                      jnp.concatenate(ps, axis=1), preferred_element_type=F32))
        finalize(acc_ref[...], jnp.concatenate(lx, axis=1))


def _diff_attention(q, kvs, lamv, sg_col, n_batch, sq, tq, tk, lam_init):
    m = q.shape[0]
    nq = sq // tq
    in_specs = [pl.BlockSpec((tq, HEAD_W), lambda b, h, i: (b * nq + i, h))]
    args = [q]
    src_lens = []
    for (k, vt) in kvs:
        sl = k.shape[0] // n_batch
        src_lens.append(sl)
        in_specs.append(pl.BlockSpec((sl, HEAD_W), lambda b, h, i: (b, h)))
        in_specs.append(pl.BlockSpec((HEAD_W, sl), lambda b, h, i: (b * N_HEADS + h, 0)))
        args += [k, vt]
    in_specs += [pl.BlockSpec(lamv.shape, lambda b, h, i: (0, 0)),
                 pl.BlockSpec((HEAD_W, 1), lambda b, h, i: (0, 0)),
                 pl.BlockSpec((HEAD_W, HEAD_W), lambda b, h, i: (0, 0))]
    gid = jnp.arange(HEAD_W) // HEAD_DIM
    args += [lamv, sg_col, (gid[:, None] == gid[None, :]).astype(BF16)]
    return pl.pallas_call(
        functools.partial(_attn_kernel, src_lens=tuple(src_lens), tq=tq, tk=tk,
                          lam_init=lam_init),
        grid=(n_batch, N_HEADS, nq),
        in_specs=in_specs,
        out_specs=pl.BlockSpec((tq, HEAD_W), lambda b, h, i: (b * nq + i, h)),
        out_shape=jax.ShapeDtypeStruct((m, N_HEADS * HEAD_W), BF16),
        scratch_shapes=[pltpu.VMEM((HEAD_W, 2 * tq), F32), pltpu.VMEM((1, HEAD_W), F32),
                        pltpu.VMEM((tk, 2 * tq), F32), pltpu.VMEM((tk, 2 * tq), F32)],
        compiler_params=_cparams(("parallel", "parallel", "arbitrary")),
        name="diff_attention",
    )(*args)


def _merge_kernel(x_ref, ao_ref, u_ref, up_ref, un_ref, gates_ref, mod_ref, dw_ref, dwb_ref,
                  lng_ref, lnb_ref, wco_ref, wao_ref, wout_ref, o_ref, ubuf, cbuf, *, tm, tps, rc):
    i = pl.program_id(0)
    first = (i % tps) == 0
    last = (i % tps) == tps - 1
    ubuf[0:CONV_HALO, :] = jnp.where(first, 0.0, up_ref[...])
    ubuf[CONV_HALO:CONV_HALO + tm, :] = u_ref[...]
    ubuf[CONV_HALO + tm:2 * CONV_HALO + tm, :] = jnp.where(last, 0.0, un_ref[...])
    shift = CONV_HALO - CONV_K // 2
    win = rc + 2 * CONV_HALO

    def chunk(r, carry):
        r0 = pl.multiple_of(r * rc, rc)
        for cb in range(D_MODEL // LANES):
            sl = slice(cb * LANES, (cb + 1) * LANES)
            w = ubuf[pl.ds(r0, win), sl]
            acc = jnp.zeros((rc, LANES), F32)
            for rot in range(8):
                wr = w if rot == 0 else pltpu.roll(w, win - rot, 0)
                for k in range(CONV_K):
                    if (k + shift) % 8 == rot:
                        a0 = (k + shift) - rot
                        acc = acc + wr[a0:a0 + rc, :] * dw_ref[k:k + 1, sl]
            cbuf[pl.ds(r0, rc), sl] = acc + dwb_ref[:, sl]
        return carry

    lax.fori_loop(0, tm // rc, chunk, 0)

    cv = cbuf[...]
    mu = jnp.mean(cv, axis=-1, keepdims=True)
    xc = cv - mu
    var = jnp.mean(xc * xc, axis=-1, keepdims=True)
    y = xc * lax.rsqrt(var + EPS) * lng_ref[...] + lnb_ref[...]
    y_conv = jnp.dot(_silu(y).astype(BF16), wco_ref[...], preferred_element_type=F32)
    y_attn = jnp.dot(ao_ref[...], wao_ref[...], preferred_element_type=F32)
    mix = (gates_ref[:, 0:D_MODEL].astype(F32) * y_attn
           + gates_ref[:, D_MODEL:2 * D_MODEL].astype(F32) * y_conv)
    out = jnp.dot(mix.astype(BF16), wout_ref[...], preferred_element_type=F32)
    o_ref[...] = x_ref[...] + mod_ref[:, 2 * D_MODEL:3 * D_MODEL] * out


def _merge(x, ao, u, gates, mod3, row_of, dw, dwb, lng, lnb, wco, wao, wout, seq, tm):
    m, d = x.shape
    tps = seq // tm
    hb = tm // CONV_HALO
    n_halo = m // CONV_HALO
    row = lambda i: (i, 0)
    const = lambda i: (0, 0)
    in_specs = [
        pl.BlockSpec((tm, d), row),
        pl.BlockSpec((tm, d), row),
        pl.BlockSpec((tm, d), row),
        pl.BlockSpec((CONV_HALO, d), lambda i: (jnp.maximum(i * hb - 1, 0), 0)),
        pl.BlockSpec((CONV_HALO, d), lambda i: (jnp.minimum((i + 1) * hb, n_halo - 1), 0)),
        pl.BlockSpec((tm, 2 * d), row),
        pl.BlockSpec((None, 1, mod3.shape[2]), lambda i: (row_of(i), 0, 0)),
        pl.BlockSpec(dw.shape, const),
        pl.BlockSpec((1, d), const), pl.BlockSpec((1, d), const), pl.BlockSpec((1, d), const),
        pl.BlockSpec((d, d), const), pl.BlockSpec((d, d), const), pl.BlockSpec((d, d), const),
    ]
    return pl.pallas_call(
        functools.partial(_merge_kernel, tm=tm, tps=tps, rc=64),
        grid=(m // tm,),
        in_specs=in_specs,
        out_specs=pl.BlockSpec((tm, d), row),
        out_shape=jax.ShapeDtypeStruct((m, d), F32),
        scratch_shapes=[pltpu.VMEM((tm + 2 * CONV_HALO, d), F32), pltpu.VMEM((tm, d), F32)],
        compiler_params=_cparams(("parallel",)),
        name="conv_merge",
    )(x, ao, u, u, u, gates, mod3, dw, dwb, lng, lnb, wco, wao, wout)


def _swiglu_tile(hb, wg_ref, wu_ref, wd_ref, n_split):
    dff = wg_ref.shape[-1]
    step = dff // n_split
    y = None
    for f0 in range(0, dff, step):
        a = jnp.dot(hb, wg_ref[:, f0:f0 + step], preferred_element_type=F32)
        b = jnp.dot(hb, wu_ref[:, f0:f0 + step], preferred_element_type=F32)
        part = jnp.dot((_silu(a) * b).astype(BF16), wd_ref[f0:f0 + step, :],
                       preferred_element_type=F32)
        y = part if y is None else y + part
    return y


def _ffn_kernel(x_ref, mod_ref, g_ref, wg_ref, wu_ref, wd_ref, o_ref, *, n_split):
    h = _norm_modulate(x_ref[...], g_ref[...], mod_ref[:, 3 * D_MODEL:4 * D_MODEL],
                       mod_ref[:, 4 * D_MODEL:5 * D_MODEL]).astype(BF16)
    y = _swiglu_tile(h, wg_ref, wu_ref, wd_ref, n_split)
    o_ref[...] = x_ref[...] + mod_ref[:, 5 * D_MODEL:6 * D_MODEL] * y


def _dense_ffn(x, mod3, row_of, g, wg, wu, wd, tm, tf):
    m, d = x.shape
    dff = wg.shape[1]
    resident = pl.Buffered(1)
    return pl.pallas_call(
        functools.partial(_ffn_kernel, n_split=dff // tf),
        grid=(m // tm,),
        in_specs=[pl.BlockSpec((tm, d), lambda i: (i, 0)),
                  pl.BlockSpec((None, 1, mod3.shape[2]), lambda i: (row_of(i), 0, 0)),
                  pl.BlockSpec((1, d), lambda i: (0, 0)),
                  pl.BlockSpec((d, dff), lambda i: (0, 0), pipeline_mode=resident),
                  pl.BlockSpec((d, dff), lambda i: (0, 0), pipeline_mode=resident),
                  pl.BlockSpec((dff, d), lambda i: (0, 0), pipeline_mode=resident)],
        out_specs=pl.BlockSpec((tm, d), lambda i: (i, 0)),
        out_shape=jax.ShapeDtypeStruct((m, d), F32),
        compiler_params=_cparams(("parallel",)),
        name="dense_ffn",
    )(x, mod3, g, wg, wu, wd)


R_E0, R_E1, R_RANK0, R_RANK1, R_W0, R_W1 = range(6)
ITEM_VALID, ITEM_FIRST, ITEM_LAST, ITEM_ZERO = 1, 2, 4, 8


def _router_kernel(x_ref, mod_ref, g_ref, wr_ref, h_ref, route_ref, cum_ref, base_ref):
    @pl.when(pl.program_id(0) == 0)
    def _():
        base_ref[...] = jnp.zeros(base_ref.shape, F32)

    h = _norm_modulate(x_ref[...], g_ref[...], mod_ref[:, 3 * D_MODEL:4 * D_MODEL],
                       mod_ref[:, 4 * D_MODEL:5 * D_MODEL])
    h_ref[...] = h.astype(BF16)
    logits = jnp.dot(h, wr_ref[...], preferred_element_type=F32, precision=lax.Precision.HIGHEST)
    tb = logits.shape[0]
    lane = lax.broadcasted_iota(jnp.int32, logits.shape, 1)
    lg = jnp.where(lane < N_EXPERTS, logits, NEG_BIG)
    m1 = jnp.max(lg, axis=1, keepdims=True)
    e0 = jnp.min(jnp.where(lg == m1, lane, LANES), axis=1, keepdims=True)
    lg2 = jnp.where(lane == e0, NEG_BIG, lg)
    m2 = jnp.max(lg2, axis=1, keepdims=True)
    e1 = jnp.min(jnp.where(lg2 == m2, lane, LANES), axis=1, keepdims=True)
    ex = jnp.exp(m2 - m1)
    w0 = 1.0 / (1.0 + ex)
    w1 = ex / (1.0 + ex)
    oh0 = lane == e0
    oh1 = lane == e1
    oh = jnp.where(oh0, 1.0, 0.0) + jnp.where(oh1, 1.0, 0.0)
    ltri = jnp.where(lax.broadcasted_iota(jnp.int32, (tb, tb), 1)
                     < lax.broadcasted_iota(jnp.int32, (tb, tb), 0), 1.0, 0.0).astype(BF16)
    rank = jnp.dot(ltri, oh.astype(BF16), preferred_element_type=F32) + base_ref[...]
    rank0 = jnp.sum(jnp.where(oh0, rank, 0.0), axis=1, keepdims=True)
    rank1 = jnp.sum(jnp.where(oh1, rank, 0.0), axis=1, keepdims=True)
    route = jnp.zeros(logits.shape, F32)
    for j, val in ((R_E0, e0.astype(F32)), (R_E1, e1.astype(F32)), (R_RANK0, rank0),
                   (R_RANK1, rank1), (R_W0, w0), (R_W1, w1)):
        route = jnp.where(lane == j, val, route)
    route_ref[...] = route
    base = base_ref[...] + jnp.sum(oh, axis=0, keepdims=True)
    base_ref[...] = base
    cum_ref[...] = base


def _router(x, mod3, row_of, g, wr_pad, tb):
    m, d = x.shape
    nb = m // tb
    return pl.pallas_call(
        _router_kernel,
        grid=(nb,),
        in_specs=[pl.BlockSpec((tb, d), lambda i: (i, 0)),
                  pl.BlockSpec((None, 1, mod3.shape[2]), lambda i: (row_of(i), 0, 0)),
                  pl.BlockSpec((1, d), lambda i: (0, 0)),
                  pl.BlockSpec((d, LANES), lambda i: (0, 0))],
        out_specs=[pl.BlockSpec((tb, d), lambda i: (i, 0)),
                   pl.BlockSpec((tb, LANES), lambda i: (i, 0)),
                   pl.BlockSpec((None, 1, LANES), lambda i: (i, 0, 0))],
        out_shape=[jax.ShapeDtypeStruct((m, d), BF16),
                   jax.ShapeDtypeStruct((m, LANES), F32),
                   jax.ShapeDtypeStruct((nb, 1, LANES), F32)],
        scratch_shapes=[pltpu.VMEM((1, LANES), F32)],
        compiler_params=_cparams(("arbitrary",)),
        name="moe_router",
    )(x, mod3, g, wr_pad)


def _moe_plan(route, cum, nb, tb, tw):
    i32 = jnp.int32
    wpt = tb // tw
    span = wpt + 1
    n_tiles = 2 * nb + N_EXPERTS
    n_items = n_tiles * wpt + N_EXPERTS * nb
    cum_in = cum[:, 0, :N_EXPERTS].astype(i32)
    cum_ex = jnp.concatenate([jnp.zeros((1, N_EXPERTS), i32), cum_in[:-1]], axis=0)
    counts = cum_in[-1]
    ntile = (counts + tb - 1) // tb
    tile_end = jnp.cumsum(ntile)
    start = (tile_end - ntile) * tb
    n_used = tile_end[-1]
    t_idx = jnp.arange(n_tiles, dtype=i32)
    tile_valid = (t_idx < n_used).astype(i32)
    tile_src = jnp.minimum(t_idx, n_used - 1)
    tile_expert = jnp.minimum(jnp.sum((tile_src[:, None] >= tile_end[None, :]).astype(i32), axis=1),
                              N_EXPERTS - 1)
    e0 = route[:, R_E0].astype(i32)
    e1 = route[:, R_E1].astype(i32)
    pos0 = start[e0] + route[:, R_RANK0].astype(i32)
    pos1 = start[e1] + route[:, R_RANK1].astype(i32)
    lo = start[None, :] + cum_ex
    hi = start[None, :] + cum_in
    w_lo = lo // tw
    npair = jnp.where(hi > lo, (hi - 1) // tw - w_lo + 1, 0)
    j = jnp.arange(span, dtype=i32)
    c_valid = j[None, None, :] < npair[:, :, None]
    c_win = w_lo[:, :, None] + j[None, None, :]
    c_blk = jnp.broadcast_to(jnp.arange(nb, dtype=i32)[:, None, None], c_valid.shape)
    jp = jnp.arange(wpt - 1, dtype=i32)
    p_win = ((start + counts + tw - 1) // tw)[:, None] + jp[None, :]
    p_valid = p_win < (tile_end * wpt)[:, None]

    def compact(valid, cols, n):
        order = jnp.argsort(jnp.logical_not(valid), stable=True)[:n]
        n_valid = jnp.sum(valid.astype(i32))
        k = jnp.minimum(jnp.arange(n, dtype=i32), n_valid - 1)
        return [c[order][k] for c in cols], jnp.arange(n, dtype=i32) < n_valid

    def flags(valid, key, extra):
        prev = jnp.concatenate([key[:1] - 1, key[:-1]])
        nxt = jnp.concatenate([key[1:], key[-1:] - 1])
        last_valid = jnp.concatenate([jnp.logical_not(valid[1:]), jnp.ones((1,), bool)])
        return (valid.astype(i32) * ITEM_VALID
                + (valid & (key != prev)).astype(i32) * ITEM_FIRST
                + (valid & ((key != nxt) | last_valid)).astype(i32) * ITEM_LAST
                + (valid & extra).astype(i32) * ITEM_ZERO)

    em = lambda a: jnp.transpose(a, (1, 0, 2)).reshape(-1)
    n_pad = N_EXPERTS * (wpt - 1)
    g_valid = jnp.concatenate([em(c_valid), p_valid.reshape(-1)])
    g_win = jnp.concatenate([em(c_win), p_win.reshape(-1)])
    g_blk = jnp.concatenate([em(c_blk), jnp.zeros((n_pad,), i32)])
    g_zero = jnp.concatenate([jnp.zeros(em(c_valid).shape, bool), jnp.ones((n_pad,), bool)])
    (g_win, g_blk, g_zero), g_ok = compact(g_valid, [g_win, g_blk, g_zero], n_items + n_pad)
    n_slots = 2 * wpt + 2 * N_EXPERTS
    s_valid = c_valid.reshape(nb, -1)
    s_order = jnp.argsort(jnp.logical_not(s_valid), axis=1, stable=True)[:, :n_slots]
    s_valid = jnp.take_along_axis(s_valid, s_order, axis=1)
    s_win = jnp.take_along_axis(c_win.reshape(nb, -1), s_order, axis=1)
    return dict(pos0=pos0, pos1=pos1, start=start, tile_src=tile_src, tile_expert=tile_expert,
                tile_valid=tile_valid, n_tiles=n_tiles, n_slots=n_slots,
                gather=(g_win, g_blk, flags(g_ok, g_win, g_zero)),
                slot_fetch=jnp.where(s_valid, s_win, 0).reshape(-1),
                slot_match=jnp.where(s_valid, s_win, -1).reshape(-1))


def _gather_kernel(it_win, it_blk, it_flag, p0_ref, p1_ref, h_ref, o_ref, *, tb, tw):
    k = pl.program_id(0)
    flag = it_flag[k]

    @pl.when((flag & ITEM_ZERO) != 0)
    def _():
        o_ref[...] = jnp.zeros(o_ref.shape, BF16)

    @pl.when(((flag & ITEM_VALID) != 0) & ((flag & ITEM_ZERO) == 0))
    def _():
        rows = lax.broadcasted_iota(jnp.int32, (tw, tb), 0) + it_win[k] * tw
        sel = jnp.where(p0_ref[...] == rows, 1.0, 0.0) + jnp.where(p1_ref[...] == rows, 1.0, 0.0)
        c = jnp.dot(sel.astype(BF16), h_ref[...], preferred_element_type=F32)

        @pl.when((flag & ITEM_FIRST) != 0)
        def _():
            o_ref[...] = c.astype(BF16)

        @pl.when((flag & ITEM_FIRST) == 0)
        def _():
            o_ref[...] = (o_ref[...].astype(F32) + c).astype(BF16)


def _moe_gather(h, plan, tb, tw):
    m, d = h.shape
    nb = m // tb
    it_win, it_blk, it_flag = plan["gather"]
    p0 = plan["pos0"].reshape(nb, 1, tb)
    p1 = plan["pos1"].reshape(nb, 1, tb)
    grid_spec = pltpu.PrefetchScalarGridSpec(
        num_scalar_prefetch=3,
        grid=(it_win.shape[0],),
        in_specs=[pl.BlockSpec((None, 1, tb), lambda k, w, b, f: (b[k], 0, 0)),
                  pl.BlockSpec((None, 1, tb), lambda k, w, b, f: (b[k], 0, 0)),
                  pl.BlockSpec((tb, d), lambda k, w, b, f: (b[k], 0))],
        out_specs=pl.BlockSpec((tw, d), lambda k, w, b, f: (w[k], 0)),
    )
    return pl.pallas_call(
        functools.partial(_gather_kernel, tb=tb, tw=tw),
        grid_spec=grid_spec,
        out_shape=jax.ShapeDtypeStruct((plan["n_tiles"] * tb, d), BF16),
        compiler_params=_cparams(("arbitrary",)),
        name="moe_gather",
    )(it_win, it_blk, it_flag, p0, p1, h)


def _expert_kernel(ts_ref, te_ref, tv_ref, x_ref, wg_ref, wu_ref, wd_ref, o_ref, *, n_split):
    @pl.when(tv_ref[pl.program_id(0)] != 0)
    def _():
        o_ref[...] = _swiglu_tile(x_ref[...], wg_ref, wu_ref, wd_ref, n_split).astype(BF16)


def _moe_experts(xs, plan, wg, wu, wd, tb, tf):
    d = xs.shape[1]
    dff = wg.shape[2]
    single = pl.Buffered(1)
    grid_spec = pltpu.PrefetchScalarGridSpec(
        num_scalar_prefetch=3,
        grid=(plan["n_tiles"],),
        in_specs=[pl.BlockSpec((tb, d), lambda t, ts, te, tv: (ts[t], 0)),
                  pl.BlockSpec((None, d, dff), lambda t, ts, te, tv: (te[t], 0, 0),
                               pipeline_mode=single),
                  pl.BlockSpec((None, d, dff), lambda t, ts, te, tv: (te[t], 0, 0),
                               pipeline_mode=single),
                  pl.BlockSpec((None, dff, d), lambda t, ts, te, tv: (te[t], 0, 0),
                               pipeline_mode=single)],
        out_specs=pl.BlockSpec((tb, d), lambda t, ts, te, tv: (ts[t], 0)),
    )
    return pl.pallas_call(
        functools.partial(_expert_kernel, n_split=dff // tf),
        grid_spec=grid_spec,
        out_shape=jax.ShapeDtypeStruct(xs.shape, BF16),
        compiler_params=_cparams(("arbitrary",)),
        name="moe_experts",
    )(plan["tile_src"], plan["tile_expert"], plan["tile_valid"], xs, wg, wu, wd)


def _combine_kernel(sf_ref, sm_ref, x_ref, mod_ref, pw_ref, *rest, tb, tw, n_slots):
    y_refs, o_ref = rest[:n_slots], rest[n_slots]
    i = pl.program_id(0)
    reps = tw // LANES
    lane_rep = lambda j: jnp.tile(pw_ref[:, j * LANES:(j + 1) * LANES], (1, reps))
    p0, p1, w0, w1 = lane_rep(0), lane_rep(1), lane_rep(2), lane_rep(3)
    col = lax.broadcasted_iota(jnp.int32, (tb, tw), 1).astype(F32)
    acc = None
    for j in range(n_slots):
        rows = col + (sm_ref[i * n_slots + j] * tw).astype(F32)
        sel = jnp.where(rows == p0, w0, 0.0) + jnp.where(rows == p1, w1, 0.0)
        c = jnp.dot(sel.astype(BF16), y_refs[j][...], preferred_element_type=F32)
        acc = c if acc is None else acc + c
    o_ref[...] = x_ref[...] + mod_ref[:, 5 * D_MODEL:6 * D_MODEL] * acc


def _moe_combine(x, mod3, row_of, route, ys, plan, tb, tw):
    m, d = x.shape
    n_slots = plan["n_slots"]
    rep = lambda v: jnp.broadcast_to(v.astype(F32)[:, None], (m, LANES))
    pw = jnp.concatenate([rep(plan["pos0"]), rep(plan["pos1"]),
                          rep(route[:, R_W0]), rep(route[:, R_W1])], axis=1)
    in_specs = [pl.BlockSpec((tb, d), lambda i, sf, sm: (i, 0)),
                pl.BlockSpec((None, 1, mod3.shape[2]), lambda i, sf, sm: (row_of(i), 0, 0)),
                pl.BlockSpec((tb, 4 * LANES), lambda i, sf, sm: (i, 0))]
    for j in range(n_slots):
        in_specs.append(pl.BlockSpec((tw, d), lambda i, sf, sm, j=j: (sf[i * n_slots + j], 0)))
    grid_spec = pltpu.PrefetchScalarGridSpec(
        num_scalar_prefetch=2,
        grid=(m // tb,),
        in_specs=in_specs,
        out_specs=pl.BlockSpec((tb, d), lambda i, sf, sm: (i, 0)),
    )
    return pl.pallas_call(
        functools.partial(_combine_kernel, tb=tb, tw=tw, n_slots=n_slots),
        grid_spec=grid_spec,
        out_shape=jax.ShapeDtypeStruct((m, d), F32),
        compiler_params=_cparams(("arbitrary",)),
        name="moe_combine",
    )(plan["slot_fetch"], plan["slot_match"], x, mod3, pw, *([ys] * n_slots))


def _moe_ffn(x, mod3, row_of, g, wr_pad, wg, wu, wd, tb, tf):
    nb = x.shape[0] // tb
    tw = min(MOE_WINDOW, tb)
    h, route, cum = _router(x, mod3, row_of, g, wr_pad, tb)
    plan = _moe_plan(route, cum, nb, tb, tw)
    xs = _moe_gather(h, plan, tb, tw)
    ys = _moe_experts(xs, plan, wg, wu, wd, tb, tf)
    return _moe_combine(x, mod3, row_of, route, ys, plan, tb, tw)


def _rope_tables(seq):
    t = jnp.arange(seq, dtype=jnp.int32)
    row = (t // GRID_W).astype(F32)
    col = (t % GRID_W).astype(F32)
    inv = ROPE_THETA ** (-jnp.arange(0, ROT_AXIS, 2, dtype=F32) / ROT_AXIS)
    ar = row[:, None] * inv
    ac = col[:, None] * inv
    ang = jnp.concatenate([ar, ar, ac, ac], axis=-1)
    ang = jnp.concatenate([ang, ang], axis=-1)
    cos, sin = jnp.cos(ang), jnp.sin(ang)
    upper = (jnp.arange(HEAD_W) & ROT_HALF) != 0
    sa = jnp.where(upper, 0.0, -sin)
    sb = jnp.where(upper, sin, 0.0)
    return cos, sa, sb


def _pick_tile(n, pref):
    t = min(n, pref)
    while n % t:
        t //= 2
    return t


def kernel(x, c, ctx, c_ctx, w_mod, b_mod, g_mix, w_in, q_norm_g, k_norm_g, lambda_q1, lambda_k1, lambda_q2, lambda_k2, subln_g, w_attn_o, dw_weight, dw_bias, conv_ln_g, conv_ln_b, w_conv_o, w_out, g_ffn, w_ff_gate, w_ff_up, w_ff_down, w_router, w_exp_gate, w_exp_up, w_exp_down):
    n_batch, seq, d = x.shape
    ctx_len = ctx.shape[1]
    depth = w_mod.shape[0]
    assert d == D_MODEL and seq % GRID_W == 0

    tm_l = _pick_tile(seq, 512)
    tm_c = _pick_tile(ctx_len, 512)
    ctx_row = n_batch
    mod_rows = -(-(n_batch + 1) // 8) * 8
    cvec = jnp.zeros((mod_rows, d), F32).at[:n_batch].set(c).at[ctx_row].set(c_ctx)

    cos, sa, sb = _rope_tables(seq)
    gid = jnp.arange(HEAD_W) // HEAD_DIM
    gsum = ((gid[:, None] == gid[None, :]).astype(F32) / HEAD_DIM).astype(BF16)

    def rope_with_gain(g_row):
        return (cos * g_row, sa * jnp.roll(g_row, -ROT_HALF, axis=1),
                sb * jnp.roll(g_row, ROT_HALF, axis=1))

    xl = x.reshape(n_batch * seq, d)
    xc = ctx.reshape(n_batch * ctx_len, d)
    lat_row = lambda i: (i * tm_l) // seq
    ctx_row_of = lambda i: ctx_row

    for l in range(depth):
        last = l == depth - 1
        lam_init = 0.8 - 0.6 * math.exp(-0.3 * l)
        mod3 = _modulation(cvec, w_mod, b_mod, l).reshape(mod_rows, 1, 6 * d)
        w_bf = w_in[l].astype(BF16)
        gmix = g_mix[l].reshape(1, d)
        qg = jnp.tile(q_norm_g[l], 2).reshape(1, HEAD_W) * Q_SCALE
        kg = jnp.tile(k_norm_g[l], 2).reshape(1, HEAD_W)
        tabs = rope_with_gain(qg) + rope_with_gain(kg)
        lamv = jnp.stack([lambda_q1[l], lambda_k1[l], lambda_q2[l], lambda_k2[l]])
        sg = subln_g[l].reshape(HEAD_W, 1)
        dw = jnp.zeros((32, d), F32).at[:CONV_K].set(dw_weight[l])
        merge_w = (dw, dw_bias[l].reshape(1, d), conv_ln_g[l].reshape(1, d),
                   conv_ln_b[l].reshape(1, d), w_conv_o[l].astype(BF16),
                   w_attn_o[l].astype(BF16), w_out[l].astype(BF16))

        q, k, v, u, gates = _in_proj_full(xl, mod3, lat_row, gmix, w_bf, qg, kg, gsum, tabs,
                                          seq, tm_l)
        if last:
            _, kc, vc = _proj(xc, mod3, ctx_row_of, gmix, w_bf, qg, kg, gsum, None,
                              ctx_len, GROUPS_KV, True, tm_c)
        else:
            qc, kc, vc, uc, gatesc = _in_proj_full(xc, mod3, ctx_row_of, gmix, w_bf, qg, kg,
                                                   gsum, None, ctx_len, tm_c)
        tk = ATTN_TK
        ao = _diff_attention(q, [(k, v), (kc, vc)], lamv, sg, n_batch, seq,
                             _pick_tile(seq, ATTN_TQ), tk, lam_init)
        xl = _merge(xl, ao, u, gates, mod3, lat_row, *merge_w, seq, tm_l)
        if not last:
            aoc = _diff_attention(qc, [(kc, vc)], lamv, sg, n_batch, ctx_len,
                                  _pick_tile(ctx_len, ATTN_TQ), tk, lam_init)
            xc = _merge(xc, aoc, uc, gatesc, mod3, ctx_row_of, *merge_w, ctx_len, tm_c)

        gf = g_ffn[l].reshape(1, d)
        i = l // 2
        if l % 2 == 0:
            wts = (w_ff_gate[i].astype(BF16), w_ff_up[i].astype(BF16), w_ff_down[i].astype(BF16))
            tf = _pick_ff_tile(wts[0].shape[1])
            xl = _dense_ffn(xl, mod3, lat_row, gf, *wts, tm_l, tf)
            if not last:
                xc = _dense_ffn(xc, mod3, ctx_row_of, gf, *wts, tm_c, tf)
        else:
            wr_pad = jnp.zeros((d, LANES), F32).at[:, :N_EXPERTS].set(w_router[i])
            wts = (w_exp_gate[i].astype(BF16), w_exp_up[i].astype(BF16),
                   w_exp_down[i].astype(BF16))
            tf = _pick_ff_tile(wts[0].shape[2])
            xl = _moe_ffn(xl, mod3, lat_row, gf, wr_pad, *wts, tm_l, tf)
            if not last:
                xc = _moe_ffn(xc, mod3, ctx_row_of, gf, wr_pad, *wts, tm_c, tf)
    return xl.reshape(n_batch, seq, d)


def _pick_ff_tile(dff):
    best = LANES
    for t in range(LANES, 1792 + 1, LANES):
        if dff % t == 0:
            best = t
    return best
```

```python
import functools
import math

import jax
import jax.numpy as jnp
from jax import lax
from jax.experimental import pallas as pl
from jax.experimental.pallas import tpu as pltpu

F32 = jnp.float32
BF16 = jnp.bfloat16

LANES = 128
D_MODEL = 1024
N_HEADS = 8
HEAD_DIM = 64
HEAD_W = 2 * HEAD_DIM
GRID_W = 64
ROT_AXIS = HEAD_DIM // 2
ROT_HALF = ROT_AXIS // 2
ROPE_THETA = 10000.0
CONV_K = 31
CONV_HALO = 16
N_EXPERTS = 8
EPS = 1e-6
N_GROUPS = 7
ATTN_TQ = 1024
ATTN_TK = 1024
MOE_WINDOW = 256
KNORM_ROWS = 512
BOUND_SLACK = 1.01
MIN_DENOM = 2.0 ** -60
VMEM_LIMIT = 52 * 1024 * 1024
Q_SCALE = (HEAD_DIM ** -0.5) * math.log2(math.e)
NEG_BIG = -1e30


def _cparams(sem):
    return pltpu.CompilerParams(dimension_semantics=sem, vmem_limit_bytes=VMEM_LIMIT)


def _sigmoid(v):
    return 0.5 * jnp.tanh(0.5 * v) + 0.5


def _silu(v):
    return v * _sigmoid(v)


def _norm_modulate(xf, g, shift, scale):
    ms = jnp.mean(xf * xf, axis=-1, keepdims=True)
    return (xf * lax.rsqrt(ms + EPS) * g) * (1.0 + scale) + shift


def _mod_kernel(c_ref, w_ref, b_ref, o_ref):
    a = _silu(c_ref[...])
    o_ref[...] = jnp.dot(a, w_ref[...], preferred_element_type=F32,
                         precision=lax.Precision.HIGHEST) + b_ref[...]


def _modulation(cvec, w_all, b_all, layer):
    rows, d = cvec.shape
    n = w_all.shape[2]
    tn = n // 4
    return pl.pallas_call(
        _mod_kernel,
        grid=(n // tn,),
        in_specs=[pl.BlockSpec((rows, d), lambda j: (0, 0)),
                  pl.BlockSpec((None, d, tn), lambda j: (layer, 0, j)),
                  pl.BlockSpec((None, 1, tn), lambda j: (layer, 0, j))],
        out_specs=pl.BlockSpec((rows, tn), lambda j: (0, j)),
        out_shape=jax.ShapeDtypeStruct((rows, n), F32),
        compiler_params=_cparams(("arbitrary",)),
        name="modulation",
    )(cvec, w_all, b_all.reshape(b_all.shape[0], 1, n))


G_Q, G_K, G_V, G_GLU_A, G_GLU_G, G_GATE_A, G_GATE_C = range(N_GROUPS)
GROUPS_QK = (G_Q, G_K)
GROUPS_REST = (G_V, G_GLU_A, G_GLU_G, G_GATE_A, G_GATE_C)
GROUPS_KV = (G_K, G_V)


def _proj_kernel(*refs, groups, from_x, rope):
    if from_x:
        x_ref, mod_ref, gmix_ref = refs[:3]
        pos = 3
    else:
        hin_ref = refs[0]
        pos = 1
    w_refs = refs[pos:pos + len(groups)]
    pos += len(groups)
    qg_ref, kg_ref, gsum_ref = refs[pos:pos + 3]
    pos += 3
    tabs = {}
    if rope:
        for g in (G_Q, G_K):
            if g in groups:
                tabs[g] = refs[pos:pos + 3]
                pos += 3
    outs = list(refs[pos:])
    if from_x:
        h = _norm_modulate(x_ref[...], gmix_ref[...], mod_ref[:, 0:D_MODEL],
                           mod_ref[:, D_MODEL:2 * D_MODEL]).astype(BF16)
        h_out = outs.pop(0)
        h_out[...] = h
    else:
        h = hin_ref[...]
    out_of = {}
    for g in groups:
        if g in (G_Q, G_K, G_V):
            out_of[g] = outs.pop(0)
    if G_GLU_A in groups:
        u_ref = outs.pop(0)
    if G_GATE_A in groups:
        gates_ref = outs.pop(0)

    def qk_norm(acc, out_ref, g_row_ref, tables):
        gsum = gsum_ref[...]
        for hb in range(N_HEADS):
            sl = slice(hb * HEAD_W, (hb + 1) * HEAD_W)
            blk = acc[:, sl]
            ms = jnp.dot((blk * blk).astype(BF16), gsum, preferred_element_type=F32)
            y = blk * lax.rsqrt(ms + EPS)
            if tables is not None:
                c_ref, sa_ref, sb_ref = tables
                y = (y * c_ref[...] + pltpu.roll(y, HEAD_W - ROT_HALF, 1) * sa_ref[...]
                     + pltpu.roll(y, ROT_HALF, 1) * sb_ref[...])
            else:
                y = y * g_row_ref[...]
            out_ref[:, sl] = y.astype(BF16)

    glu_a = None
    for g, w_ref in zip(groups, w_refs):
        acc = jnp.dot(h, w_ref[...], preferred_element_type=F32)
        if g == G_Q:
            qk_norm(acc, out_of[g], qg_ref, tabs.get(g))
        elif g == G_K:
            qk_norm(acc, out_of[g], kg_ref, tabs.get(g))
        elif g == G_V:
            out_of[g][...] = acc.T.astype(BF16)
        elif g == G_GLU_A:
            glu_a = acc
        elif g == G_GLU_G:
            u_ref[...] = glu_a * _sigmoid(acc)
        elif g == G_GATE_A:
            gates_ref[:, 0:D_MODEL] = _sigmoid(acc).astype(BF16)
        elif g == G_GATE_C:
            gates_ref[:, D_MODEL:2 * D_MODEL] = _sigmoid(acc).astype(BF16)


def _proj(src, mod3, row_of, gmix, w_bf, qg, kg, gsum, rope_tabs, seq, groups, from_x, tm):
    m, d = src.shape
    tps = seq // tm
    rope = rope_tabs is not None
    const = lambda i: (0, 0)
    row = lambda i: (i, 0)
    if from_x:
        in_specs = [pl.BlockSpec((tm, d), row),
                    pl.BlockSpec((None, 1, mod3.shape[2]), lambda i: (row_of(i), 0, 0)),
                    pl.BlockSpec((1, d), const)]
        args = [src, mod3, gmix]
    else:
        in_specs = [pl.BlockSpec((tm, d), row)]
        args = [src]
    for g in groups:
        in_specs.append(pl.BlockSpec((d, D_MODEL), lambda i, g=g: (0, g)))
        args.append(w_bf)
    in_specs += [pl.BlockSpec((1, HEAD_W), const), pl.BlockSpec((1, HEAD_W), const),
                 pl.BlockSpec((HEAD_W, HEAD_W), const)]
    args += [qg, kg, gsum]
    if rope:
        for g, t3 in ((G_Q, rope_tabs[0:3]), (G_K, rope_tabs[3:6])):
            if g in groups:
                for t in t3:
                    in_specs.append(pl.BlockSpec((tm, HEAD_W), lambda i: (i % tps, 0)))
                    args.append(t)
    row_spec = pl.BlockSpec((tm, D_MODEL), row)
    rows_bf = jax.ShapeDtypeStruct((m, D_MODEL), BF16)
    out_specs, out_shape = [], []
    if from_x:
        out_specs.append(row_spec)
        out_shape.append(rows_bf)
    for g in groups:
        if g in (G_Q, G_K):
            out_specs.append(row_spec)
            out_shape.append(rows_bf)
        elif g == G_V:
            out_specs.append(pl.BlockSpec((D_MODEL, tm), lambda i: (i // tps, i % tps)))
            out_shape.append(jax.ShapeDtypeStruct(((m // seq) * D_MODEL, seq), BF16))
    if G_GLU_A in groups:
        out_specs.append(row_spec)
        out_shape.append(jax.ShapeDtypeStruct((m, D_MODEL), F32))
    if G_GATE_A in groups:
        out_specs.append(pl.BlockSpec((tm, 2 * D_MODEL), row))
        out_shape.append(jax.ShapeDtypeStruct((m, 2 * D_MODEL), BF16))
    return pl.pallas_call(
        functools.partial(_proj_kernel, groups=groups, from_x=from_x, rope=rope),
        grid=(m // tm,),
        in_specs=in_specs,
        out_specs=out_specs,
        out_shape=out_shape,
        compiler_params=_cparams(("parallel",)),
        name="in_proj",
    )(*args)


def _in_proj_full(x, mod3, row_of, gmix, w_bf, qg, kg, gsum, rope_tabs, seq, tm):
    h, q, k = _proj(x, mod3, row_of, gmix, w_bf, qg, kg, gsum, rope_tabs, seq, GROUPS_QK, True, tm)
    vt, u, gates = _proj(h, mod3, row_of, gmix, w_bf, qg, kg, gsum, None, seq, GROUPS_REST,
                         False, tm)
    return q, k, vt, u, gates


def _attn_kernel(*refs, src_lens, tq, tk, lam_init):
    n_src = len(src_lens)
    q_ref = refs[0]
    kv_refs = refs[1:1 + 2 * n_src]
    lamv_ref, sg_ref, ksum_ref, o_ref, acc_ref, kmax_ref = refs[1 + 2 * n_src:7 + 2 * n_src]
    s_refs = refs[7 + 2 * n_src:9 + 2 * n_src]
    n_cb = 2 * tq // LANES

    @pl.when(pl.program_id(2) == 0)
    def _():
        best = jnp.zeros((1, HEAD_W), F32)
        for si in range(n_src):
            for off in range(0, src_lens[si], KNORM_ROWS):
                rows = min(KNORM_ROWS, src_lens[si] - off)
                kf = kv_refs[2 * si][off:off + rows, :].astype(F32)
                n2 = jnp.dot((kf * kf).astype(BF16), ksum_ref[...], preferred_element_type=F32)
                best = jnp.maximum(best, jnp.max(n2, axis=0, keepdims=True))
        kmax_ref[...] = jnp.sqrt(best) * BOUND_SLACK

    q = q_ref[...]
    lane = lax.broadcasted_iota(jnp.int32, q.shape, 1)
    zero = jnp.zeros_like(q)
    q2 = jnp.concatenate([jnp.where(lane < HEAD_DIM, q, zero),
                          jnp.where(lane >= HEAD_DIM, q, zero)], axis=0)
    chunks = [(si, off, min(tk, src_lens[si])) for si in range(n_src)
              for off in range(0, src_lens[si], min(tk, src_lens[si]))]

    def scores(si, off, tkk):
        kc = kv_refs[2 * si][pl.ds(off, tkk), :]
        return lax.dot_general(kc, q2, (((1,), (1,)), ((), ())), preferred_element_type=F32)

    def finalize(acc, l):
        lv = lamv_ref[...]
        lam = (jnp.exp(jnp.sum(lv[0:1] * lv[1:2], axis=1, keepdims=True))
               - jnp.exp(jnp.sum(lv[2:3] * lv[3:4], axis=1, keepdims=True)) + lam_init)
        inv = 1.0 / l
        o_t = acc[:, 0:tq] * inv[:, 0:tq] - lam * (acc[:, tq:2 * tq] * inv[:, tq:2 * tq])
        ms = jnp.mean(o_t * o_t, axis=0, keepdims=True)
        y_t = o_t * lax.rsqrt(ms + EPS) * (sg_ref[...] * (1.0 - lam_init))
        o_ref[...] = y_t.T.astype(BF16)

    q2f = q2.astype(F32)
    qn2 = lax.dot_general(jnp.ones((8, HEAD_W), BF16), (q2f * q2f).astype(BF16),
                          (((1,), (1,)), ((), ())), preferred_element_type=F32)[0:1, :]
    kmax = kmax_ref[...]
    kmax_row = jnp.concatenate([jnp.broadcast_to(kmax[:, 0:1], (1, tq)),
                                jnp.broadcast_to(kmax[:, HEAD_DIM:HEAD_DIM + 1], (1, tq))], axis=1)
    bound = jnp.sqrt(qn2) * BOUND_SLACK * kmax_row
    s_refs[0][0:chunks[0][2], :] = scores(*chunks[0])
    l = jnp.zeros((1, 2 * tq), F32)
    for ci, (si, off, tkk) in enumerate(chunks):
        if ci + 1 < len(chunks):
            s_refs[(ci + 1) % 2][0:chunks[ci + 1][2], :] = scores(*chunks[ci + 1])
        p = jnp.exp2(s_refs[ci % 2][0:tkk, :] - bound)
        l = l + jnp.sum(p, axis=0, keepdims=True)
        pv = jnp.dot(kv_refs[2 * si + 1][:, pl.ds(off, tkk)], p.astype(BF16),
                     preferred_element_type=F32)
        if ci == 0:
            acc_ref[...] = pv
        else:
            acc_ref[...] += pv
    healthy = jnp.min(l) >= MIN_DENOM

    @pl.when(healthy)
    def _():
        finalize(acc_ref[...], l)

    @pl.when(jnp.logical_not(healthy))
    def _():
        acc_ref[...] = jnp.zeros(acc_ref.shape, F32)
        m = [jnp.full((1, LANES), NEG_BIG, F32) for _ in range(n_cb)]
        lx = [jnp.zeros((1, LANES), F32) for _ in range(n_cb)]
        for si, off, tkk in chunks:
            s = scores(si, off, tkk)
            alphas, ps = [], []
            for cb in range(n_cb):
                s_cb = s[:, cb * LANES:(cb + 1) * LANES]
                m_new = jnp.maximum(m[cb], jnp.max(s_cb, axis=0, keepdims=True))
                alpha = jnp.exp2(m[cb] - m_new)
                p_cb = jnp.exp2(s_cb - m_new)
                lx[cb] = alpha * lx[cb] + jnp.sum(p_cb, axis=0, keepdims=True)
                alphas.append(alpha)
                ps.append(p_cb.astype(BF16))
                m[cb] = m_new
            acc_ref[...] = (jnp.concatenate(alphas, axis=1) * acc_ref[...]
                            + jnp.dot(kv_refs[2 * si + 1][:, pl.ds(off, tkk)],
                                      jnp.concatenate(ps, axis=1), preferred_element_type=F32))
        finalize(acc_ref[...], jnp.concatenate(lx, axis=1))


def _diff_attention(q, kvs, lamv, sg_col, n_batch, sq, tq, tk, lam_init):
    m = q.shape[0]
    nq = sq // tq
    in_specs = [pl.BlockSpec((tq, HEAD_W), lambda b, h, i: (b * nq + i, h))]
    args = [q]
    src_lens = []
    for (k, vt) in kvs:
        sl = k.shape[0] // n_batch
        src_lens.append(sl)
        in_specs.append(pl.BlockSpec((sl, HEAD_W), lambda b, h, i: (b, h)))
        in_specs.append(pl.BlockSpec((HEAD_W, sl), lambda b, h, i: (b * N_HEADS + h, 0)))
        args += [k, vt]
    in_specs += [pl.BlockSpec(lamv.shape, lambda b, h, i: (0, 0)),
                 pl.BlockSpec((HEAD_W, 1), lambda b, h, i: (0, 0)),
                 pl.BlockSpec((HEAD_W, HEAD_W), lambda b, h, i: (0, 0))]
    gid = jnp.arange(HEAD_W) // HEAD_DIM
    args += [lamv, sg_col, (gid[:, None] == gid[None, :]).astype(BF16)]
    return pl.pallas_call(
        functools.partial(_attn_kernel, src_lens=tuple(src_lens), tq=tq, tk=tk,
                          lam_init=lam_init),
        grid=(n_batch, N_HEADS, nq),
        in_specs=in_specs,
        out_specs=pl.BlockSpec((tq, HEAD_W), lambda b, h, i: (b * nq + i, h)),
        out_shape=jax.ShapeDtypeStruct((m, N_HEADS * HEAD_W), BF16),
        scratch_shapes=[pltpu.VMEM((HEAD_W, 2 * tq), F32), pltpu.VMEM((1, HEAD_W), F32),
                        pltpu.VMEM((tk, 2 * tq), F32), pltpu.VMEM((tk, 2 * tq), F32)],
        compiler_params=_cparams(("parallel", "parallel", "arbitrary")),
        name="diff_attention",
    )(*args)


def _merge_kernel(x_ref, ao_ref, u_ref, up_ref, un_ref, gates_ref, mod_ref, dw_ref, dwb_ref,
                  lng_ref, lnb_ref, wco_ref, wao_ref, wout_ref, o_ref, ubuf, cbuf, *, tm, tps, rc):
    i = pl.program_id(0)
    first = (i % tps) == 0
    last = (i % tps) == tps - 1
    ubuf[0:CONV_HALO, :] = jnp.where(first, 0.0, up_ref[...])
    ubuf[CONV_HALO:CONV_HALO + tm, :] = u_ref[...]
    ubuf[CONV_HALO + tm:2 * CONV_HALO + tm, :] = jnp.where(last, 0.0, un_ref[...])
    shift = CONV_HALO - CONV_K // 2
    win = rc + 2 * CONV_HALO

    def chunk(r, carry):
        r0 = pl.multiple_of(r * rc, rc)
        for cb in range(D_MODEL // LANES):
            sl = slice(cb * LANES, (cb + 1) * LANES)
            w = ubuf[pl.ds(r0, win), sl]
            acc = jnp.zeros((rc, LANES), F32)
            for rot in range(8):
                wr = w if rot == 0 else pltpu.roll(w, win - rot, 0)
                for k in range(CONV_K):
                    if (k + shift) % 8 == rot:
                        a0 = (k + shift) - rot
                        acc = acc + wr[a0:a0 + rc, :] * dw_ref[k:k + 1, sl]
            cbuf[pl.ds(r0, rc), sl] = acc + dwb_ref[:, sl]
        return carry

    lax.fori_loop(0, tm // rc, chunk, 0)

    cv = cbuf[...]
    mu = jnp.mean(cv, axis=-1, keepdims=True)
    xc = cv - mu
    var = jnp.mean(xc * xc, axis=-1, keepdims=True)
    y = xc * lax.rsqrt(var + EPS) * lng_ref[...] + lnb_ref[...]
    y_conv = jnp.dot(_silu(y).astype(BF16), wco_ref[...], preferred_element_type=F32)
    y_attn = jnp.dot(ao_ref[...], wao_ref[...], preferred_element_type=F32)
    mix = (gates_ref[:, 0:D_MODEL].astype(F32) * y_attn
           + gates_ref[:, D_MODEL:2 * D_MODEL].astype(F32) * y_conv)
    out = jnp.dot(mix.astype(BF16), wout_ref[...], preferred_element_type=F32)
    o_ref[...] = x_ref[...] + mod_ref[:, 2 * D_MODEL:3 * D_MODEL] * out


def _merge(x, ao, u, gates, mod3, row_of, dw, dwb, lng, lnb, wco, wao, wout, seq, tm):
    m, d = x.shape
    tps = seq // tm
    hb = tm // CONV_HALO
    n_halo = m // CONV_HALO
    row = lambda i: (i, 0)
    const = lambda i: (0, 0)
    in_specs = [
        pl.BlockSpec((tm, d), row),
        pl.BlockSpec((tm, d), row),
        pl.BlockSpec((tm, d), row),
        pl.BlockSpec((CONV_HALO, d), lambda i: (jnp.maximum(i * hb - 1, 0), 0)),
        pl.BlockSpec((CONV_HALO, d), lambda i: (jnp.minimum((i + 1) * hb, n_halo - 1), 0)),
        pl.BlockSpec((tm, 2 * d), row),
        pl.BlockSpec((None, 1, mod3.shape[2]), lambda i: (row_of(i), 0, 0)),
        pl.BlockSpec(dw.shape, const),
        pl.BlockSpec((1, d), const), pl.BlockSpec((1, d), const), pl.BlockSpec((1, d), const),
        pl.BlockSpec((d, d), const), pl.BlockSpec((d, d), const), pl.BlockSpec((d, d), const),
    ]
    return pl.pallas_call(
        functools.partial(_merge_kernel, tm=tm, tps=tps, rc=64),
        grid=(m // tm,),
        in_specs=in_specs,
        out_specs=pl.BlockSpec((tm, d), row),
        out_shape=jax.ShapeDtypeStruct((m, d), F32),
        scratch_shapes=[pltpu.VMEM((tm + 2 * CONV_HALO, d), F32), pltpu.VMEM((tm, d), F32)],
        compiler_params=_cparams(("parallel",)),
        name="conv_merge",
    )(x, ao, u, u, u, gates, mod3, dw, dwb, lng, lnb, wco, wao, wout)


def _swiglu_tile(hb, wg_ref, wu_ref, wd_ref, n_split):
    dff = wg_ref.shape[-1]
    step = dff // n_split
    y = None
    for f0 in range(0, dff, step):
        a = jnp.dot(hb, wg_ref[:, f0:f0 + step], preferred_element_type=F32)
        b = jnp.dot(hb, wu_ref[:, f0:f0 + step], preferred_element_type=F32)
        part = jnp.dot((_silu(a) * b).astype(BF16), wd_ref[f0:f0 + step, :],
                       preferred_element_type=F32)
        y = part if y is None else y + part
    return y


def _ffn_kernel(x_ref, mod_ref, g_ref, wg_ref, wu_ref, wd_ref, o_ref, *, n_split):
    h = _norm_modulate(x_ref[...], g_ref[...], mod_ref[:, 3 * D_MODEL:4 * D_MODEL],
                       mod_ref[:, 4 * D_MODEL:5 * D_MODEL]).astype(BF16)
    y = _swiglu_tile(h, wg_ref, wu_ref, wd_ref, n_split)
    o_ref[...] = x_ref[...] + mod_ref[:, 5 * D_MODEL:6 * D_MODEL] * y


def _dense_ffn(x, mod3, row_of, g, wg, wu, wd, tm):
    m, d = x.shape
    dff = wg.shape[1]
    resident = pl.Buffered(1)
    return pl.pallas_call(
        functools.partial(_ffn_kernel, n_split=1),
        grid=(m // tm,),
        in_specs=[pl.BlockSpec((tm, d), lambda i: (i, 0)),
                  pl.BlockSpec((None, 1, mod3.shape[2]), lambda i: (row_of(i), 0, 0)),
                  pl.BlockSpec((1, d), lambda i: (0, 0)),
                  pl.BlockSpec((d, dff), lambda i: (0, 0), pipeline_mode=resident),
                  pl.BlockSpec((d, dff), lambda i: (0, 0), pipeline_mode=resident),
                  pl.BlockSpec((dff, d), lambda i: (0, 0), pipeline_mode=resident)],
        out_specs=pl.BlockSpec((tm, d), lambda i: (i, 0)),
        out_shape=jax.ShapeDtypeStruct((m, d), F32),
        compiler_params=_cparams(("parallel",)),
        name="dense_ffn",
    )(x, mod3, g, wg, wu, wd)


R_E0, R_E1, R_RANK0, R_RANK1, R_W0, R_W1 = range(6)
ITEM_VALID, ITEM_FIRST, ITEM_LAST, ITEM_ZERO = 1, 2, 4, 8


def _router_kernel(x_ref, mod_ref, g_ref, wr_ref, h_ref, route_ref, cum_ref, base_ref):
    @pl.when(pl.program_id(0) == 0)
    def _():
        base_ref[...] = jnp.zeros(base_ref.shape, F32)

    h = _norm_modulate(x_ref[...], g_ref[...], mod_ref[:, 3 * D_MODEL:4 * D_MODEL],
                       mod_ref[:, 4 * D_MODEL:5 * D_MODEL])
    h_ref[...] = h.astype(BF16)
    logits = jnp.dot(h, wr_ref[...], preferred_element_type=F32, precision=lax.Precision.HIGHEST)
    tb = logits.shape[0]
    lane = lax.broadcasted_iota(jnp.int32, logits.shape, 1)
    lg = jnp.where(lane < N_EXPERTS, logits, NEG_BIG)
    m1 = jnp.max(lg, axis=1, keepdims=True)
    e0 = jnp.min(jnp.where(lg == m1, lane, LANES), axis=1, keepdims=True)
    lg2 = jnp.where(lane == e0, NEG_BIG, lg)
    m2 = jnp.max(lg2, axis=1, keepdims=True)
    e1 = jnp.min(jnp.where(lg2 == m2, lane, LANES), axis=1, keepdims=True)
    ex = jnp.exp(m2 - m1)
    w0 = 1.0 / (1.0 + ex)
    w1 = ex / (1.0 + ex)
    oh0 = lane == e0
    oh1 = lane == e1
    oh = jnp.where(oh0, 1.0, 0.0) + jnp.where(oh1, 1.0, 0.0)
    ltri = jnp.where(lax.broadcasted_iota(jnp.int32, (tb, tb), 1)
                     < lax.broadcasted_iota(jnp.int32, (tb, tb), 0), 1.0, 0.0).astype(BF16)
    rank = jnp.dot(ltri, oh.astype(BF16), preferred_element_type=F32) + base_ref[...]
    rank0 = jnp.sum(jnp.where(oh0, rank, 0.0), axis=1, keepdims=True)
    rank1 = jnp.sum(jnp.where(oh1, rank, 0.0), axis=1, keepdims=True)
    route = jnp.zeros(logits.shape, F32)
    for j, val in ((R_E0, e0.astype(F32)), (R_E1, e1.astype(F32)), (R_RANK0, rank0),
                   (R_RANK1, rank1), (R_W0, w0), (R_W1, w1)):
        route = jnp.where(lane == j, val, route)
    route_ref[...] = route
    base = base_ref[...] + jnp.sum(oh, axis=0, keepdims=True)
    base_ref[...] = base
    cum_ref[...] = base


def _router(x, mod3, row_of, g, wr_pad, tb):
    m, d = x.shape
    nb = m // tb
    return pl.pallas_call(
        _router_kernel,
        grid=(nb,),
        in_specs=[pl.BlockSpec((tb, d), lambda i: (i, 0)),
                  pl.BlockSpec((None, 1, mod3.shape[2]), lambda i: (row_of(i), 0, 0)),
                  pl.BlockSpec((1, d), lambda i: (0, 0)),
                  pl.BlockSpec((d, LANES), lambda i: (0, 0))],
        out_specs=[pl.BlockSpec((tb, d), lambda i: (i, 0)),
                   pl.BlockSpec((tb, LANES), lambda i: (i, 0)),
                   pl.BlockSpec((None, 1, LANES), lambda i: (i, 0, 0))],
        out_shape=[jax.ShapeDtypeStruct((m, d), BF16),
                   jax.ShapeDtypeStruct((m, LANES), F32),
                   jax.ShapeDtypeStruct((nb, 1, LANES), F32)],
        scratch_shapes=[pltpu.VMEM((1, LANES), F32)],
        compiler_params=_cparams(("arbitrary",)),
        name="moe_router",
    )(x, mod3, g, wr_pad)


def _moe_plan(route, cum, nb, tb, tw):
    i32 = jnp.int32
    wpt = tb // tw
    span = wpt + 1
    n_tiles = 2 * nb + N_EXPERTS
    n_items = n_tiles * wpt + N_EXPERTS * nb
    cum_in = cum[:, 0, :N_EXPERTS].astype(i32)
    cum_ex = jnp.concatenate([jnp.zeros((1, N_EXPERTS), i32), cum_in[:-1]], axis=0)
    counts = cum_in[-1]
    ntile = (counts + tb - 1) // tb
    tile_end = jnp.cumsum(ntile)
    start = (tile_end - ntile) * tb
    n_used = tile_end[-1]
    t_idx = jnp.arange(n_tiles, dtype=i32)
    tile_valid = (t_idx < n_used).astype(i32)
    tile_src = jnp.minimum(t_idx, n_used - 1)
    tile_expert = jnp.minimum(jnp.sum((tile_src[:, None] >= tile_end[None, :]).astype(i32), axis=1),
                              N_EXPERTS - 1)
    e0 = route[:, R_E0].astype(i32)
    e1 = route[:, R_E1].astype(i32)
    pos0 = start[e0] + route[:, R_RANK0].astype(i32)
    pos1 = start[e1] + route[:, R_RANK1].astype(i32)
    lo = start[None, :] + cum_ex
    hi = start[None, :] + cum_in
    w_lo = lo // tw
    npair = jnp.where(hi > lo, (hi - 1) // tw - w_lo + 1, 0)
    j = jnp.arange(span, dtype=i32)
    c_valid = j[None, None, :] < npair[:, :, None]
    c_win = w_lo[:, :, None] + j[None, None, :]
    c_blk = jnp.broadcast_to(jnp.arange(nb, dtype=i32)[:, None, None], c_valid.shape)
    jp = jnp.arange(wpt - 1, dtype=i32)
    p_win = ((start + counts + tw - 1) // tw)[:, None] + jp[None, :]
    p_valid = p_win < (tile_end * wpt)[:, None]

    def compact(valid, cols, n):
        order = jnp.argsort(jnp.logical_not(valid), stable=True)[:n]
        n_valid = jnp.sum(valid.astype(i32))
        k = jnp.minimum(jnp.arange(n, dtype=i32), n_valid - 1)
        return [c[order][k] for c in cols], jnp.arange(n, dtype=i32) < n_valid

    def flags(valid, key, extra):
        prev = jnp.concatenate([key[:1] - 1, key[:-1]])
        nxt = jnp.concatenate([key[1:], key[-1:] - 1])
        last_valid = jnp.concatenate([jnp.logical_not(valid[1:]), jnp.ones((1,), bool)])
        return (valid.astype(i32) * ITEM_VALID
                + (valid & (key != prev)).astype(i32) * ITEM_FIRST
                + (valid & ((key != nxt) | last_valid)).astype(i32) * ITEM_LAST
                + (valid & extra).astype(i32) * ITEM_ZERO)

    em = lambda a: jnp.transpose(a, (1, 0, 2)).reshape(-1)
    n_pad = N_EXPERTS * (wpt - 1)
    g_valid = jnp.concatenate([em(c_valid), p_valid.reshape(-1)])
    g_win = jnp.concatenate([em(c_win), p_win.reshape(-1)])
    g_blk = jnp.concatenate([em(c_blk), jnp.zeros((n_pad,), i32)])
    g_zero = jnp.concatenate([jnp.zeros(em(c_valid).shape, bool), jnp.ones((n_pad,), bool)])
    (g_win, g_blk, g_zero), g_ok = compact(g_valid, [g_win, g_blk, g_zero], n_items + n_pad)
    n_slots = 2 * wpt + 2 * N_EXPERTS
    s_valid = c_valid.reshape(nb, -1)
    s_order = jnp.argsort(jnp.logical_not(s_valid), axis=1, stable=True)[:, :n_slots]
    s_valid = jnp.take_along_axis(s_valid, s_order, axis=1)
    s_win = jnp.take_along_axis(c_win.reshape(nb, -1), s_order, axis=1)
    return dict(pos0=pos0, pos1=pos1, start=start, tile_src=tile_src, tile_expert=tile_expert,
                tile_valid=tile_valid, n_tiles=n_tiles, n_slots=n_slots,
                gather=(g_win, g_blk, flags(g_ok, g_win, g_zero)),
                slot_fetch=jnp.where(s_valid, s_win, 0).reshape(-1),
                slot_match=jnp.where(s_valid, s_win, -1).reshape(-1))


def _gather_kernel(it_win, it_blk, it_flag, p0_ref, p1_ref, h_ref, o_ref, *, tb, tw):
    k = pl.program_id(0)
    flag = it_flag[k]

    @pl.when((flag & ITEM_ZERO) != 0)
    def _():
        o_ref[...] = jnp.zeros(o_ref.shape, BF16)

    @pl.when(((flag & ITEM_VALID) != 0) & ((flag & ITEM_ZERO) == 0))
    def _():
        rows = lax.broadcasted_iota(jnp.int32, (tw, tb), 0) + it_win[k] * tw
        sel = jnp.where(p0_ref[...] == rows, 1.0, 0.0) + jnp.where(p1_ref[...] == rows, 1.0, 0.0)
        c = jnp.dot(sel.astype(BF16), h_ref[...], preferred_element_type=F32)

        @pl.when((flag & ITEM_FIRST) != 0)
        def _():
            o_ref[...] = c.astype(BF16)

        @pl.when((flag & ITEM_FIRST) == 0)
        def _():
            o_ref[...] = (o_ref[...].astype(F32) + c).astype(BF16)


def _moe_gather(h, plan, tb, tw):
    m, d = h.shape
    nb = m // tb
    it_win, it_blk, it_flag = plan["gather"]
    p0 = plan["pos0"].reshape(nb, 1, tb)
    p1 = plan["pos1"].reshape(nb, 1, tb)
    grid_spec = pltpu.PrefetchScalarGridSpec(
        num_scalar_prefetch=3,
        grid=(it_win.shape[0],),
        in_specs=[pl.BlockSpec((None, 1, tb), lambda k, w, b, f: (b[k], 0, 0)),
                  pl.BlockSpec((None, 1, tb), lambda k, w, b, f: (b[k], 0, 0)),
                  pl.BlockSpec((tb, d), lambda k, w, b, f: (b[k], 0))],
        out_specs=pl.BlockSpec((tw, d), lambda k, w, b, f: (w[k], 0)),
    )
    return pl.pallas_call(
        functools.partial(_gather_kernel, tb=tb, tw=tw),
        grid_spec=grid_spec,
        out_shape=jax.ShapeDtypeStruct((plan["n_tiles"] * tb, d), BF16),
        compiler_params=_cparams(("arbitrary",)),
        name="moe_gather",
    )(it_win, it_blk, it_flag, p0, p1, h)


def _expert_kernel(ts_ref, te_ref, tv_ref, x_ref, wg_ref, wu_ref, wd_ref, o_ref, *, n_split):
    @pl.when(tv_ref[pl.program_id(0)] != 0)
    def _():
        o_ref[...] = _swiglu_tile(x_ref[...], wg_ref, wu_ref, wd_ref, n_split).astype(BF16)


def _moe_experts(xs, plan, wg, wu, wd, tb, tf):
    d = xs.shape[1]
    dff = wg.shape[2]
    single = pl.Buffered(1)
    grid_spec = pltpu.PrefetchScalarGridSpec(
        num_scalar_prefetch=3,
        grid=(plan["n_tiles"],),
        in_specs=[pl.BlockSpec((tb, d), lambda t, ts, te, tv: (ts[t], 0)),
                  pl.BlockSpec((None, d, dff), lambda t, ts, te, tv: (te[t], 0, 0),
                               pipeline_mode=single),
                  pl.BlockSpec((None, d, dff), lambda t, ts, te, tv: (te[t], 0, 0),
                               pipeline_mode=single),
                  pl.BlockSpec((None, dff, d), lambda t, ts, te, tv: (te[t], 0, 0))],
        out_specs=pl.BlockSpec((tb, d), lambda t, ts, te, tv: (ts[t], 0)),
    )
    return pl.pallas_call(
        functools.partial(_expert_kernel, n_split=dff // tf),
        grid_spec=grid_spec,
        out_shape=jax.ShapeDtypeStruct(xs.shape, BF16),
        compiler_params=_cparams(("arbitrary",)),
        name="moe_experts",
    )(plan["tile_src"], plan["tile_expert"], plan["tile_valid"], xs, wg, wu, wd)


def _combine_kernel(sf_ref, sm_ref, x_ref, mod_ref, pw_ref, *rest, tb, tw, n_slots):
    y_refs, o_ref = rest[:n_slots], rest[n_slots]
    i = pl.program_id(0)
    reps = tw // LANES
    lane_rep = lambda j: jnp.tile(pw_ref[:, j * LANES:(j + 1) * LANES], (1, reps))
    p0, p1, w0, w1 = lane_rep(0), lane_rep(1), lane_rep(2), lane_rep(3)
    col = lax.broadcasted_iota(jnp.int32, (tb, tw), 1).astype(F32)
    acc = None
    for j in range(n_slots):
        rows = col + (sm_ref[i * n_slots + j] * tw).astype(F32)
        sel = jnp.where(rows == p0, w0, 0.0) + jnp.where(rows == p1, w1, 0.0)
        c = jnp.dot(sel.astype(BF16), y_refs[j][...], preferred_element_type=F32)
        acc = c if acc is None else acc + c
    o_ref[...] = x_ref[...] + mod_ref[:, 5 * D_MODEL:6 * D_MODEL] * acc


def _moe_combine(x, mod3, row_of, route, ys, plan, tb, tw):
    m, d = x.shape
    n_slots = plan["n_slots"]
    rep = lambda v: jnp.broadcast_to(v.astype(F32)[:, None], (m, LANES))
    pw = jnp.concatenate([rep(plan["pos0"]), rep(plan["pos1"]),
                          rep(route[:, R_W0]), rep(route[:, R_W1])], axis=1)
    in_specs = [pl.BlockSpec((tb, d), lambda i, sf, sm: (i, 0)),
                pl.BlockSpec((None, 1, mod3.shape[2]), lambda i, sf, sm: (row_of(i), 0, 0)),
                pl.BlockSpec((tb, 4 * LANES), lambda i, sf, sm: (i, 0))]
    for j in range(n_slots):
        in_specs.append(pl.BlockSpec((tw, d), lambda i, sf, sm, j=j: (sf[i * n_slots + j], 0)))
    grid_spec = pltpu.PrefetchScalarGridSpec(
        num_scalar_prefetch=2,
        grid=(m // tb,),
        in_specs=in_specs,
        out_specs=pl.BlockSpec((tb, d), lambda i, sf, sm: (i, 0)),
    )
    return pl.pallas_call(
        functools.partial(_combine_kernel, tb=tb, tw=tw, n_slots=n_slots),
        grid_spec=grid_spec,
        out_shape=jax.ShapeDtypeStruct((m, d), F32),
        compiler_params=_cparams(("arbitrary",)),
        name="moe_combine",
    )(plan["slot_fetch"], plan["slot_match"], x, mod3, pw, *([ys] * n_slots))


def _moe_ffn(x, mod3, row_of, g, wr_pad, wg, wu, wd, tb, tf):
    nb = x.shape[0] // tb
    tw = min(MOE_WINDOW, tb)
    h, route, cum = _router(x, mod3, row_of, g, wr_pad, tb)
    plan = _moe_plan(route, cum, nb, tb, tw)
    xs = _moe_gather(h, plan, tb, tw)
    ys = _moe_experts(xs, plan, wg, wu, wd, tb, tf)
    return _moe_combine(x, mod3, row_of, route, ys, plan, tb, tw)


def _rope_tables(seq):
    t = jnp.arange(seq, dtype=jnp.int32)
    row = (t // GRID_W).astype(F32)
    col = (t % GRID_W).astype(F32)
    inv = ROPE_THETA ** (-jnp.arange(0, ROT_AXIS, 2, dtype=F32) / ROT_AXIS)
    ar = row[:, None] * inv
    ac = col[:, None] * inv
    ang = jnp.concatenate([ar, ar, ac, ac], axis=-1)
    ang = jnp.concatenate([ang, ang], axis=-1)
    cos, sin = jnp.cos(ang), jnp.sin(ang)
    upper = (jnp.arange(HEAD_W) & ROT_HALF) != 0
    sa = jnp.where(upper, 0.0, -sin)
    sb = jnp.where(upper, sin, 0.0)
    return cos, sa, sb


def _pick_tile(n, pref):
    t = min(n, pref)
    while n % t:
        t //= 2
    return t


def kernel(x, c, ctx, c_ctx, w_mod, b_mod, g_mix, w_in, q_norm_g, k_norm_g, lambda_q1, lambda_k1, lambda_q2, lambda_k2, subln_g, w_attn_o, dw_weight, dw_bias, conv_ln_g, conv_ln_b, w_conv_o, w_out, g_ffn, w_ff_gate, w_ff_up, w_ff_down, w_router, w_exp_gate, w_exp_up, w_exp_down):
    n_batch, seq, d = x.shape
    ctx_len = ctx.shape[1]
    depth = w_mod.shape[0]
    assert d == D_MODEL and seq % GRID_W == 0

    tm_l = _pick_tile(seq, 512)
    tm_c = _pick_tile(ctx_len, 512)
    ctx_row = n_batch
    mod_rows = -(-(n_batch + 1) // 8) * 8
    cvec = jnp.zeros((mod_rows, d), F32).at[:n_batch].set(c).at[ctx_row].set(c_ctx)

    cos, sa, sb = _rope_tables(seq)
    gid = jnp.arange(HEAD_W) // HEAD_DIM
    gsum = ((gid[:, None] == gid[None, :]).astype(F32) / HEAD_DIM).astype(BF16)

    def rope_with_gain(g_row):
        return (cos * g_row, sa * jnp.roll(g_row, -ROT_HALF, axis=1),
                sb * jnp.roll(g_row, ROT_HALF, axis=1))

    xl = x.reshape(n_batch * seq, d)
    xc = ctx.reshape(n_batch * ctx_len, d)
    lat_row = lambda i: (i * tm_l) // seq
    ctx_row_of = lambda i: ctx_row

    for l in range(depth):
        last = l == depth - 1
        lam_init = 0.8 - 0.6 * math.exp(-0.3 * l)
        mod3 = _modulation(cvec, w_mod, b_mod, l).reshape(mod_rows, 1, 6 * d)
        w_bf = w_in[l].astype(BF16)
        gmix = g_mix[l].reshape(1, d)
        qg = jnp.tile(q_norm_g[l], 2).reshape(1, HEAD_W) * Q_SCALE
        kg = jnp.tile(k_norm_g[l], 2).reshape(1, HEAD_W)
        tabs = rope_with_gain(qg) + rope_with_gain(kg)
        lamv = jnp.stack([lambda_q1[l], lambda_k1[l], lambda_q2[l], lambda_k2[l]])
        sg = subln_g[l].reshape(HEAD_W, 1)
        dw = jnp.zeros((32, d), F32).at[:CONV_K].set(dw_weight[l])
        merge_w = (dw, dw_bias[l].reshape(1, d), conv_ln_g[l].reshape(1, d),
                   conv_ln_b[l].reshape(1, d), w_conv_o[l].astype(BF16),
                   w_attn_o[l].astype(BF16), w_out[l].astype(BF16))

        q, k, v, u, gates = _in_proj_full(xl, mod3, lat_row, gmix, w_bf, qg, kg, gsum, tabs,
                                          seq, tm_l)
        if last:
            _, kc, vc = _proj(xc, mod3, ctx_row_of, gmix, w_bf, qg, kg, gsum, None,
                              ctx_len, GROUPS_KV, True, tm_c)
        else:
            qc, kc, vc, uc, gatesc = _in_proj_full(xc, mod3, ctx_row_of, gmix, w_bf, qg, kg,
                                                   gsum, None, ctx_len, tm_c)
        tk = ATTN_TK
        ao = _diff_attention(q, [(k, v), (kc, vc)], lamv, sg, n_batch, seq,
                             _pick_tile(seq, ATTN_TQ), tk, lam_init)
        xl = _merge(xl, ao, u, gates, mod3, lat_row, *merge_w, seq, tm_l)
        if not last:
            aoc = _diff_attention(qc, [(kc, vc)], lamv, sg, n_batch, ctx_len,
                                  _pick_tile(ctx_len, ATTN_TQ), tk, lam_init)
            xc = _merge(xc, aoc, uc, gatesc, mod3, ctx_row_of, *merge_w, ctx_len, tm_c)

        gf = g_ffn[l].reshape(1, d)
        i = l // 2
        if l % 2 == 0:
            wts = (w_ff_gate[i].astype(BF16), w_ff_up[i].astype(BF16), w_ff_down[i].astype(BF16))
            xl = _dense_ffn(xl, mod3, lat_row, gf, *wts, tm_l)
            if not last:
                xc = _dense_ffn(xc, mod3, ctx_row_of, gf, *wts, tm_c)
        else:
            wr_pad = jnp.zeros((d, LANES), F32).at[:, :N_EXPERTS].set(w_router[i])
            wts = (w_exp_gate[i].astype(BF16), w_exp_up[i].astype(BF16),
                   w_exp_down[i].astype(BF16))
            tf = _pick_ff_tile(wts[0].shape[2])
            xl = _moe_ffn(xl, mod3, lat_row, gf, wr_pad, *wts, tm_l, tf)
            if not last:
                xc = _moe_ffn(xc, mod3, ctx_row_of, gf, wr_pad, *wts, tm_c, tf)
    return xl.reshape(n_batch, seq, d)


def _pick_ff_tile(dff):
    best = LANES
    for t in range(LANES, 1792 + 1, LANES):
        if dff % t == 0:
            best = t
    return best
```
